```python
import jax, jax.numpy as jnp
from jax import lax
import numpy as np

D_MODEL = 2048
BATCH = 4
SEQ = 4096
DEPTH = 1

MLSTM_HEADS = 4
MLSTM_DK = 256
MLSTM_DV = 256
MLSTM_CONV = 4
MLSTM_CHUNK = 64
RWKV_HEADS = 16
RWKV_HEAD = 64
RWKV_WIDTH = RWKV_HEADS * RWKV_HEAD
W_LORA = 96
A_LORA = 96
G_LORA = 256
PEER_HEADS = 8
N_KEYS = 128
N_EXPERTS = N_KEYS * N_KEYS
PEER_TOPK = 16
D_KEY = 256
EXPERT_BLOCK = 128
RMS_EPS = 1e-6
GN_EPS = 64e-5
L2_EPS = 1e-12

MLSTM_QK = MLSTM_HEADS * MLSTM_DK
MLSTM_V = MLSTM_HEADS * MLSTM_DV
MLSTM_COLS = 2 * MLSTM_QK + 2 * MLSTM_V + 2 * MLSTM_HEADS
RWKV_COLS = 3 * RWKV_WIDTH + W_LORA + A_LORA + G_LORA
GATE_COLS = 2 * D_MODEL
D_IN = MLSTM_COLS + RWKV_COLS + GATE_COLS

kernel_name = "hybrid_mlstm_rwkv7_peer_block"


def rms_norm(x, gain):
    xf = x.astype(jnp.float32)
    y = xf * lax.rsqrt(jnp.mean(xf * xf, axis=-1, keepdims=True) + RMS_EPS)
    return (y * gain.astype(jnp.float32)).astype(x.dtype)


def split_columns(p, sizes):
    offsets = np.cumsum(np.array(sizes))[:-1].tolist()
    return jnp.split(p, offsets, axis=-1)


def token_shift(p):
    return jnp.pad(p, ((0, 0), (1, 0), (0, 0)))[:, :-1]


def causal_depthwise_conv(p, w):
    K, C = w.shape
    return lax.conv_general_dilated(
        p, w[:, None, :].astype(p.dtype), window_strides=(1,), padding=[(K - 1, 0)],
        dimension_numbers=('NWC', 'WIO', 'NWC'), feature_group_count=C)


def mlstm_chunkwise(q, k, v, ig, lf):
    B, H, T, DK = q.shape
    DV = v.shape[-1]
    L = MLSTM_CHUNK
    NC = T // L
    q = q * DK ** -0.5
    causal = jnp.tril(jnp.ones((L, L), dtype=bool))

    def chunks(t):
        return jnp.moveaxis(t.reshape(B, H, NC, L, *t.shape[3:]), 2, 0)

    def step(carry, xs):
        C, n, m = carry
        qc, kc, vc, igc, lfc = xs
        b = jnp.cumsum(lfc, axis=-1)
        dmat = b[..., :, None] - b[..., None, :] + igc[..., None, :]
        dmat = jnp.where(causal, dmat, -jnp.inf)
        inter = b + m[..., None]
        m_t = jnp.maximum(inter, jnp.max(dmat, axis=-1))
        s = jnp.einsum('bhtd,bhsd->bhts', qc, kc) * jnp.exp(dmat - m_t[..., None])
        w_inter = jnp.exp(inter - m_t)
        num = (jnp.einsum('bhts,bhsv->bhtv', s, vc)
               + w_inter[..., None] * jnp.einsum('bhtd,bhdv->bhtv', qc, C))
        den = jnp.sum(s, axis=-1) + w_inter * jnp.einsum('bhtd,bhd->bht', qc, n)
        h = num / jnp.maximum(jnp.abs(den), jnp.exp(-m_t))[..., None]
        g_end = b[..., -1:] - b + igc
        m_new = jnp.maximum(b[..., -1] + m, jnp.max(g_end, axis=-1))
        decay = jnp.exp(b[..., -1] + m - m_new)
        ws = jnp.exp(g_end - m_new[..., None])
        C_new = decay[..., None, None] * C + jnp.einsum('bhsd,bhsv->bhdv', kc * ws[..., None], vc)
        n_new = decay[..., None] * n + jnp.einsum('bhsd,bhs->bhd', kc, ws)
        return (C_new, n_new, m_new), h

    init = (jnp.zeros((B, H, DK, DV), jnp.float32),
            jnp.zeros((B, H, DK), jnp.float32),
            jnp.zeros((B, H), jnp.float32))
    _, h = lax.scan(step, init, tuple(chunks(t) for t in (q, k, v, ig, lf)))
    return jnp.moveaxis(h, 0, 2).reshape(B, H, T, DV)


def mlstm_branch(p, conv_w, b_i, b_f):
    B, T, _ = p.shape
    qk, v, o, i_pre, f_pre = split_columns(p, (2 * MLSTM_QK, MLSTM_V, MLSTM_V, MLSTM_HEADS, MLSTM_HEADS))
    qk = jax.nn.silu(causal_depthwise_conv(qk, conv_w))
    q, k = jnp.split(qk, 2, axis=-1)

    def heads(t):
        return t.reshape(B, T, MLSTM_HEADS, -1).transpose(0, 2, 1, 3).astype(jnp.float32)

    ig = (i_pre + b_i).astype(jnp.float32).transpose(0, 2, 1)
    lf = jax.nn.log_sigmoid((f_pre + b_f).astype(jnp.float32)).transpose(0, 2, 1)
    h = mlstm_chunkwise(heads(q), heads(k), heads(v), ig, lf)
    h = h.transpose(0, 2, 1, 3).reshape(B, T, MLSTM_V).astype(p.dtype)
    return jax.nn.sigmoid(o) * h


def wkv7_scan(r, decay, k, v, aa, bb):
    B, T, H, N = r.shape
    xs = tuple(jnp.moveaxis(t, 1, 0) for t in (r, decay, k, v, aa, bb))

    def step(S, inp):
        r_t, w_t, k_t, v_t, a_t, b_t = inp
        sa = jnp.einsum('bhvk,bhk->bhv', S, a_t)
        S = (S * w_t[:, :, None, :] + sa[..., None] * b_t[:, :, None, :]
             + v_t[..., None] * k_t[:, :, None, :])
        return S, jnp.einsum('bhvk,bhk->bhv', S, r_t)

    _, y = lax.scan(step, jnp.zeros((B, H, N, N), jnp.float32), xs)
    return jnp.moveaxis(y, 0, 1)


def rwkv7_branch(p, mu, w0, w2, a0, a2, g2, k_k, k_a, r_k, ln_w, ln_b):
    B, T, _ = p.shape
    p = p + (token_shift(p) - p) * mu
    r, k, v, wl, al, gl = split_columns(p, (RWKV_WIDTH, RWKV_WIDTH, RWKV_WIDTH, W_LORA, A_LORA, G_LORA))
    w = -jax.nn.softplus(-(w0 + jnp.tanh(wl) @ w2).astype(jnp.float32)) - 0.5
    decay = jnp.exp(-jnp.exp(w))
    a = jax.nn.sigmoid((a0 + al @ a2).astype(jnp.float32))
    g = jax.nn.sigmoid(gl) @ g2

    def heads(t):
        return t.astype(jnp.float32).reshape(B, T, RWKV_HEADS, RWKV_HEAD)

    kk = heads(k * k_k)
    kk = kk / jnp.maximum(jnp.sqrt(jnp.sum(kk * kk, axis=-1, keepdims=True)), L2_EPS)
    k = k.astype(jnp.float32) * (1.0 + (a - 1.0) * k_a)
    rh, kh, vh = heads(r), heads(k), heads(v)
    y = wkv7_scan(rh, heads(decay), kh, vh, -kk, kk * heads(a))
    mean = jnp.mean(y, axis=-1, keepdims=True)
    var = jnp.mean(jnp.square(y - mean), axis=-1, keepdims=True)
    y = ((y - mean) * lax.rsqrt(var + GN_EPS)).reshape(B, T, RWKV_WIDTH)
    y = y * ln_w.astype(jnp.float32) + ln_b.astype(jnp.float32)
    bonus = jnp.sum(rh * kh * r_k.astype(jnp.float32), axis=-1, keepdims=True) * vh
    y = y + bonus.reshape(B, T, RWKV_WIDTH)
    return y.astype(p.dtype) * g


def peer_ffn(xn, w_query, sub_keys, expert_u, expert_v):
    B, T, D = xn.shape
    NT = B * T
    xt = xn.reshape(NT, D)
    q = (xt @ w_query).reshape(NT, PEER_HEADS, 2, D_KEY // 2).astype(jnp.float32)
    s = jnp.einsum('nhpd,pkd->nhpk', q, sub_keys.astype(jnp.float32))
    v1, i1 = lax.top_k(s[:, :, 0], PEER_TOPK)
    v2, i2 = lax.top_k(s[:, :, 1], PEER_TOPK)
    cand = (v1[..., :, None] + v2[..., None, :]).reshape(NT, PEER_HEADS, PEER_TOPK * PEER_TOPK)
    best, ci = lax.top_k(cand, PEER_TOPK)
    e1 = jnp.take_along_axis(i1, ci // PEER_TOPK, axis=-1)
    e2 = jnp.take_along_axis(i2, ci % PEER_TOPK, axis=-1)
    idx = e1 * N_KEYS + e2
    gate = jax.nn.softmax(best, axis=-1)
    NB = NT // EXPERT_BLOCK
    HK = PEER_HEADS * PEER_TOPK
    xb = xt.reshape(NB, EXPERT_BLOCK, D)
    ib = idx.reshape(NB, EXPERT_BLOCK, HK)
    gb = gate.reshape(NB, EXPERT_BLOCK, HK)

    def block(args):
        xc, ic, gc = args
        act = jax.nn.gelu(jnp.einsum('cd,ced->ce', xc, expert_u[ic]), approximate=False)
        return jnp.einsum('ce,ced->cd', (gc * act).astype(xc.dtype), expert_v[ic])

    return lax.map(block, (xb, ib, gb)).reshape(B, T, D)


def setup_inputs(seed: int = 0) -> dict:
    key = jax.random.key(seed)
    ks = jax.random.split(key, 32)
    L = DEPTH

    def nrm(k, shape, scale):
        return jax.random.normal(k, shape, jnp.float32) * scale

    return {
        "x": nrm(ks[0], (BATCH, SEQ, D_MODEL), 1.0),
        "norm_mix_gain": 1.0 + nrm(ks[1], (L, D_MODEL), 0.05),
        "w_in": nrm(ks[2], (L, D_MODEL, D_IN), D_MODEL ** -0.5),
        "mlstm_conv": nrm(ks[3], (L, MLSTM_CONV, 2 * MLSTM_QK), MLSTM_CONV ** -0.5),
        "mlstm_b_i": nrm(ks[4], (L, MLSTM_HEADS), 0.1),
        "mlstm_b_f": jnp.linspace(3.0, 6.0, MLSTM_HEADS, dtype=jnp.float32)[None] + nrm(ks[5], (L, MLSTM_HEADS), 0.1),
        "rwkv_mu": jax.random.uniform(ks[6], (L, RWKV_COLS), jnp.float32),
        "rwkv_w0": nrm(ks[7], (L, RWKV_WIDTH), 0.5),
        "rwkv_w2": nrm(ks[8], (L, W_LORA, RWKV_WIDTH), W_LORA ** -0.5),
        "rwkv_a0": nrm(ks[9], (L, RWKV_WIDTH), 0.1),
        "rwkv_a2": nrm(ks[10], (L, A_LORA, RWKV_WIDTH), A_LORA ** -0.5),
        "rwkv_g2": nrm(ks[11], (L, G_LORA, RWKV_WIDTH), G_LORA ** -0.5),
        "rwkv_k_k": 0.85 + nrm(ks[12], (L, RWKV_WIDTH), 0.1),
        "rwkv_k_a": 1.0 + nrm(ks[13], (L, RWKV_WIDTH), 0.1),
        "rwkv_r_k": nrm(ks[14], (L, RWKV_HEADS, RWKV_HEAD), 0.1),
        "rwkv_ln_w": 1.0 + nrm(ks[15], (L, RWKV_WIDTH), 0.05),
        "rwkv_ln_b": nrm(ks[16], (L, RWKV_WIDTH), 0.01),
        "proj_mlstm": nrm(ks[17], (L, MLSTM_V, D_MODEL), MLSTM_V ** -0.5),
        "proj_rwkv": nrm(ks[18], (L, RWKV_WIDTH, D_MODEL), RWKV_WIDTH ** -0.5),
        "w_out": nrm(ks[19], (L, D_MODEL, D_MODEL), D_MODEL ** -0.5),
        "norm_ffn_gain": 1.0 + nrm(ks[20], (L, D_MODEL), 0.05),
        "peer_w_query": nrm(ks[21], (L, D_MODEL, PEER_HEADS * D_KEY), D_MODEL ** -0.5),
        "peer_sub_keys": nrm(ks[22], (L, 2, N_KEYS, D_KEY // 2), (D_KEY // 2) ** -0.5),
        "peer_u": nrm(ks[23], (L, N_EXPERTS, D_MODEL), D_MODEL ** -0.5),
        "peer_v": nrm(ks[24], (L, N_EXPERTS, D_MODEL), 0.5 * PEER_HEADS ** -0.5),
        "norm_final_gain": 1.0 + nrm(ks[25], (D_MODEL,), 0.05),
    }


def reference(x, norm_mix_gain, w_in, mlstm_conv, mlstm_b_i, mlstm_b_f, rwkv_mu, rwkv_w0, rwkv_w2,
              rwkv_a0, rwkv_a2, rwkv_g2, rwkv_k_k, rwkv_k_a, rwkv_r_k, rwkv_ln_w, rwkv_ln_b,
              proj_mlstm, proj_rwkv, w_out, norm_ffn_gain, peer_w_query, peer_sub_keys,
              peer_u, peer_v, norm_final_gain):
    h = x
    for l in range(DEPTH):
        xn = rms_norm(h, norm_mix_gain[l])
        p = xn @ w_in[l]
        p_m, p_r, p_g = split_columns(p, (MLSTM_COLS, RWKV_COLS, GATE_COLS))
        y_m = mlstm_branch(p_m, mlstm_conv[l], mlstm_b_i[l], mlstm_b_f[l])
        y_r = rwkv7_branch(p_r, rwkv_mu[l], rwkv_w0[l], rwkv_w2[l], rwkv_a0[l], rwkv_a2[l],
                           rwkv_g2[l], rwkv_k_k[l], rwkv_k_a[l], rwkv_r_k[l], rwkv_ln_w[l], rwkv_ln_b[l])
        g_m, g_r = jnp.split(jax.nn.sigmoid(p_g), 2, axis=-1)
        mixed = g_m * (y_m @ proj_mlstm[l]) + g_r * (y_r @ proj_rwkv[l])
        h = h + mixed @ w_out[l]
        h = h + peer_ffn(rms_norm(h, norm_ffn_gain[l]), peer_w_query[l], peer_sub_keys[l],
                         peer_u[l], peer_v[l])
    return rms_norm(h, norm_final_gain)
```

```python
import functools

import jax
import jax.numpy as jnp
from jax import lax
from jax.experimental import pallas as pl
from jax.experimental.pallas import tpu as pltpu

F32 = jnp.float32
BF16 = jnp.bfloat16

LANES = 128
SUBLANES = 8
VMEM_LIMIT_BYTES = 48 * 1024 * 1024

MLSTM_HEADS = 4
MLSTM_DK = 256
MLSTM_CONV = 4
MLSTM_CHUNK = 64
RWKV_HEADS = 16
RWKV_HEAD = 64
RWKV_WIDTH = RWKV_HEADS * RWKV_HEAD
W_LORA = 96
A_LORA = 96
G_LORA = 256
PEER_HEADS = 8
N_KEYS = 128
PEER_TOPK = 16
D_KEY = 256
RMS_EPS = 1e-6
GN_EPS = 64e-5
L2_EPS = 1e-12

MLSTM_QK = MLSTM_HEADS * MLSTM_DK
MLSTM_V = MLSTM_QK
LORA_PAD = 128

COL_QK = 0
COL_V = COL_QK + 2 * MLSTM_QK
COL_O = COL_V + MLSTM_V
COL_IF = COL_O + MLSTM_V
MLSTM_PACKED = COL_IF + LANES
RWKV_PACKED = 3 * RWKV_WIDTH + 2 * LORA_PAD + G_LORA


def _cparams(sem):
    return pltpu.CompilerParams(dimension_semantics=sem, vmem_limit_bytes=VMEM_LIMIT_BYTES)


def _split2(x):
    hi = x.astype(BF16)
    lo = (x - hi.astype(F32)).astype(BF16)
    return hi, lo


def _split3(x):
    hi = x.astype(BF16)
    r1 = x - hi.astype(F32)
    mid = r1.astype(BF16)
    lo = (r1 - mid.astype(F32)).astype(BF16)
    return hi, mid, lo


def _dot(a, b):
    return jnp.dot(a, b, preferred_element_type=F32)


def _dot_exact_rhs(x, ones_bf16):
    hi, mid, lo = _split3(x)
    return _dot(hi, ones_bf16) + _dot(mid, ones_bf16) + _dot(lo, ones_bf16)


def _norm_matmul_kernel(x_ref, g_ref, w_ref, o_ref, xn_ref):
    @pl.when(pl.program_id(1) == 0)
    def _():
        x = x_ref[...]
        ms = jnp.mean(x * x, axis=-1, keepdims=True)
        xn_ref[...] = (x * lax.rsqrt(ms + RMS_EPS) * g_ref[...]).astype(BF16)

    o_ref[...] = _dot(xn_ref[...], w_ref[...])


def norm_matmul(x, gain, w_bf16, *, tm=512, tn=384):
    M, K = x.shape
    N = w_bf16.shape[1]
    tm = min(tm, M)
    return pl.pallas_call(
        _norm_matmul_kernel,
        grid=(M // tm, N // tn),
        in_specs=[
            pl.BlockSpec((tm, K), lambda i, j: (i, 0)),
            pl.BlockSpec((1, K), lambda i, j: (0, 0)),
            pl.BlockSpec((K, tn), lambda i, j: (0, j)),
        ],
        out_specs=pl.BlockSpec((tm, tn), lambda i, j: (i, j)),
        out_shape=jax.ShapeDtypeStruct((M, N), F32),
        scratch_shapes=[pltpu.VMEM((tm, K), BF16)],
        compiler_params=_cparams(("parallel", "arbitrary")),
        name="norm_matmul",
    )(x, gain.reshape(1, K), w_bf16)


def _mlstm_kernel(qk_ref, v_ref, o_ref, if_ref, conv_ref, bias_ref, y_ref,
                  ext_ref, c_ref, n_ref, m_ref):
    L = MLSTM_CHUNK
    DK = MLSTM_DK
    step = pl.program_id(1)

    @pl.when(step == 0)
    def _():
        ext_ref[0:SUBLANES, :] = jnp.zeros((SUBLANES, 2 * MLSTM_QK), F32)
        c_ref[...] = jnp.zeros_like(c_ref)
        n_ref[...] = jnp.zeros_like(n_ref)
        m_ref[...] = jnp.zeros_like(m_ref)

    ext_ref[SUBLANES:SUBLANES + L, :] = qk_ref[0]
    acc = jnp.zeros((L, 2 * MLSTM_QK), F32)
    for j in range(MLSTM_CONV):
        off = SUBLANES - (MLSTM_CONV - 1) + j
        acc = acc + ext_ref[off:off + L, :] * conv_ref[j:j + 1, :]
    ext_ref[0:SUBLANES, :] = qk_ref[0, L - SUBLANES:L, :]
    qk = acc * jax.nn.sigmoid(acc)

    row = lax.broadcasted_iota(jnp.int32, (L, L), 0)
    col = lax.broadcasted_iota(jnp.int32, (L, L), 1)
    causal = col <= row
    eye = col == row

    def to_row(x_col):
        return jnp.sum(jnp.where(eye, x_col, 0.0), axis=0, keepdims=True)

    gates = if_ref[0]
    for h in range(MLSTM_HEADS):
        q = qk[:, h * DK:(h + 1) * DK] * (DK ** -0.5)
        k = qk[:, MLSTM_QK + h * DK:MLSTM_QK + (h + 1) * DK]
        v = v_ref[0, :, h * DK:(h + 1) * DK]
        ig_col = gates[:, h:h + 1] + bias_ref[0:1, h:h + 1]
        lf_col = jax.nn.log_sigmoid(
            gates[:, MLSTM_HEADS + h:MLSTM_HEADS + h + 1]
            + bias_ref[1:2, h:h + 1])
        lf_row = to_row(lf_col)
        ig_row = to_row(ig_col)
        b_col = jnp.sum(jnp.where(causal, lf_row, 0.0), axis=1, keepdims=True)
        b_row = to_row(b_col)
        b_last = b_col[L - 1:L, :]
        m_prev = m_ref[h:h + 1, 0:1]
        C = c_ref[h]
        n_row = n_ref[h:h + 1, :]

        dmat = jnp.where(causal, b_col - b_row + ig_row, -jnp.inf)
        inter = b_col + m_prev
        m_t = jnp.maximum(inter, jnp.max(dmat, axis=1, keepdims=True))
        qb = q.astype(BF16)
        kb = k.astype(BF16)
        vb = v.astype(BF16)
        s = lax.dot_general(qb, kb, (((1,), (1,)), ((), ())),
                            preferred_element_type=F32) * jnp.exp(dmat - m_t)
        w_inter = jnp.exp(inter - m_t)
        num = _dot(s.astype(BF16), vb) + w_inter * _dot(qb, C.astype(BF16))
        qn = jnp.sum(q * n_row, axis=1, keepdims=True)
        den = jnp.sum(s, axis=1, keepdims=True) + w_inter * qn
        hh = num / jnp.maximum(jnp.abs(den), jnp.exp(-m_t))
        o = o_ref[0, :, h * DK:(h + 1) * DK]
        y_ref[0, :, h * DK:(h + 1) * DK] = (jax.nn.sigmoid(o) * hh).astype(y_ref.dtype)

        g_end = b_last - b_col + ig_col
        m_new = jnp.maximum(b_last + m_prev, jnp.max(g_end, axis=0, keepdims=True))
        decay = jnp.exp(b_last + m_prev - m_new)
        ws = jnp.exp(g_end - m_new)
        kw = k * ws
        c_ref[h] = decay * C + lax.dot_general(
            kw.astype(BF16), vb, (((0,), (0,)), ((), ())), preferred_element_type=F32)
        n_ref[h:h + 1, :] = decay * n_row + jnp.sum(kw, axis=0, keepdims=True)
        m_ref[h:h + 1, :] = jnp.broadcast_to(m_new, (1, LANES))


def mlstm_branch(p, conv_w, b_i, b_f):
    B, T, _ = p.shape
    L = MLSTM_CHUNK
    bias = jnp.zeros((SUBLANES, LANES), F32)
    bias = bias.at[0, :MLSTM_HEADS].set(b_i).at[1, :MLSTM_HEADS].set(b_f)
    nqk = 2 * MLSTM_QK
    return pl.pallas_call(
        _mlstm_kernel,
        grid=(B, T // L),
        in_specs=[
            pl.BlockSpec((1, L, nqk), lambda b, c: (b, c, COL_QK // nqk)),
            pl.BlockSpec((1, L, MLSTM_V), lambda b, c: (b, c, COL_V // MLSTM_V)),
            pl.BlockSpec((1, L, MLSTM_V), lambda b, c: (b, c, COL_O // MLSTM_V)),
            pl.BlockSpec((1, L, LANES), lambda b, c: (b, c, COL_IF // LANES)),
            pl.BlockSpec((MLSTM_CONV, nqk), lambda b, c: (0, 0)),
            pl.BlockSpec((SUBLANES, LANES), lambda b, c: (0, 0)),
        ],
        out_specs=pl.BlockSpec((1, L, MLSTM_V), lambda b, c: (b, c, 0)),
        out_shape=jax.ShapeDtypeStruct((B, T, MLSTM_V), BF16),
        scratch_shapes=[
            pltpu.VMEM((SUBLANES + L, nqk), F32),
            pltpu.VMEM((MLSTM_HEADS, MLSTM_DK, MLSTM_DK), F32),
            pltpu.VMEM((SUBLANES, MLSTM_DK), F32),
            pltpu.VMEM((SUBLANES, LANES), F32),
        ],
        compiler_params=_cparams(("parallel", "arbitrary")),
        name="mlstm",
    )(p, p, p, p, conv_w, bias)


def _head_sum_matrix(group):
    r = lax.broadcasted_iota(jnp.int32, (LANES, LANES), 0) // group
    c = lax.broadcasted_iota(jnp.int32, (LANES, LANES), 1) // group
    return jnp.where(r == c, 1.0, 0.0).astype(BF16)


def _rwkv_pre_kernel(p_ref, prev_ref, mu_ref, vec_ref, w2_ref, a2_ref, g2_ref,
                     r_ref, w_ref, k_ref, v_ref, aa_ref, bb_ref, g_ref):
    W = RWKV_WIDTH
    tb = p_ref.shape[1]
    p = p_ref[0]
    last = prev_ref[0, SUBLANES - 1:SUBLANES, :]
    last = jnp.where(pl.program_id(1) == 0, 0.0, last)
    rid = lax.broadcasted_iota(jnp.int32, (tb, 1), 0)
    shifted = jnp.where(rid == 0, last, pltpu.roll(p, 1, 0))
    p = p + (shifted - p) * mu_ref[...]

    r = p[:, 0:W]
    k = p[:, W:2 * W]
    v = p[:, 2 * W:3 * W]
    wl = p[:, 3 * W:3 * W + LORA_PAD]
    al = p[:, 3 * W + LORA_PAD:3 * W + 2 * LORA_PAD]
    gl = p[:, 3 * W + 2 * LORA_PAD:]
    w0 = vec_ref[0:1, :]
    a0 = vec_ref[1:2, :]
    k_k = vec_ref[2:3, :]
    k_a = vec_ref[3:4, :]

    wx = -(w0 + _dot(jnp.tanh(wl).astype(BF16), w2_ref[...]))
    softplus = jnp.maximum(wx, 0.0) + jnp.log1p(jnp.exp(-jnp.abs(wx)))
    w = -softplus - 0.5
    decay = jnp.exp(-jnp.exp(w))
    a = jax.nn.sigmoid(a0 + _dot(al.astype(BF16), a2_ref[...]))
    g = _dot(jax.nn.sigmoid(gl).astype(BF16), g2_ref[...])

    kk = k * k_k
    ones = _head_sum_matrix(RWKV_HEAD)
    sq = kk * kk
    ss = jnp.concatenate(
        [_dot_exact_rhs(sq[:, j * LANES:(j + 1) * LANES], ones) for j in range(W // LANES)],
        axis=1)
    kk = kk / jnp.maximum(jnp.sqrt(ss), L2_EPS)

    r_ref[0] = r
    w_ref[0] = decay
    k_ref[0] = k * (1.0 + (a - 1.0) * k_a)
    v_ref[0] = v
    aa_ref[0] = -kk
    bb_ref[0] = kk * a
    g_ref[0] = g


def rwkv_pre(p, mu, w0, w2, a0, a2, g2, k_k, k_a, *, tb=256):
    B, T, _ = p.shape
    W = RWKV_WIDTH
    tb = min(tb, T)
    mu_p = jnp.zeros((1, RWKV_PACKED), F32)
    mu_p = mu_p.at[0, :3 * W].set(mu[:3 * W])
    mu_p = mu_p.at[0, 3 * W:3 * W + W_LORA].set(mu[3 * W:3 * W + W_LORA])
    mu_p = mu_p.at[0, 3 * W + LORA_PAD:3 * W + LORA_PAD + A_LORA].set(
        mu[3 * W + W_LORA:3 * W + W_LORA + A_LORA])
    mu_p = mu_p.at[0, 3 * W + 2 * LORA_PAD:].set(mu[3 * W + W_LORA + A_LORA:])
    vec = jnp.zeros((SUBLANES, W), F32)
    vec = vec.at[0].set(w0).at[1].set(a0).at[2].set(k_k).at[3].set(k_a)
    w2_p = jnp.zeros((LORA_PAD, W), F32).at[:W_LORA].set(w2).astype(BF16)
    a2_p = jnp.zeros((LORA_PAD, W), F32).at[:A_LORA].set(a2).astype(BF16)
    nprev = tb // SUBLANES
    out = jax.ShapeDtypeStruct((B, T, W), F32)
    ospec = pl.BlockSpec((1, tb, W), lambda b, i: (b, i, 0))
    return pl.pallas_call(
        _rwkv_pre_kernel,
        grid=(B, T // tb),
        in_specs=[
            pl.BlockSpec((1, tb, RWKV_PACKED), lambda b, i: (b, i, 0)),
            pl.BlockSpec((1, SUBLANES, RWKV_PACKED),
                         lambda b, i: (b, jnp.maximum(i * nprev - 1, 0), 0)),
            pl.BlockSpec((1, RWKV_PACKED), lambda b, i: (0, 0)),
            pl.BlockSpec((SUBLANES, W), lambda b, i: (0, 0)),
            pl.BlockSpec((LORA_PAD, W), lambda b, i: (0, 0)),
            pl.BlockSpec((LORA_PAD, W), lambda b, i: (0, 0)),
            pl.BlockSpec((G_LORA, W), lambda b, i: (0, 0)),
        ],
        out_specs=[ospec] * 7,
        out_shape=[out] * 7,
        compiler_params=_cparams(("parallel", "parallel")),
        name="rwkv_pre",
    )(p, p, mu_p, vec, w2_p, a2_p, g2.astype(BF16))


def _seg_sum(x, lo_half):
    s0 = jnp.sum(jnp.where(lo_half, x, 0.0), axis=1, keepdims=True)
    s1 = jnp.sum(jnp.where(lo_half, 0.0, x), axis=1, keepdims=True)
    return jnp.where(lo_half, s0, s1)


def _rwkv_scan_kernel(r_ref, w_ref, k_ref, v_ref, aa_ref, bb_ref, y_ref, s_ref, *, pairs):
    N = RWKV_HEAD
    tb = r_ref.shape[1]

    @pl.when(pl.program_id(2) == 0)
    def _():
        s_ref[...] = jnp.zeros_like(s_ref)

    lane = lax.broadcasted_iota(jnp.int32, (N, LANES), 1)
    sub = lax.broadcasted_iota(jnp.int32, (N, LANES), 0)
    lo_half = lane < N
    diag = (lane % N) == sub

    def body(t8, states):
        base = pl.multiple_of(t8 * SUBLANES, SUBLANES)
        new_states = []
        for q in range(pairs):
            S = states[q]
            sl = (0, pl.ds(base, SUBLANES), slice(q * LANES, (q + 1) * LANES))
            a8, b8, w8, k8, r8, v8 = (ref[sl] for ref in (aa_ref, bb_ref, w_ref, k_ref, r_ref, v_ref))
            rows = []
            for j in range(SUBLANES):
                sa = _seg_sum(S * a8[j:j + 1], lo_half)
                vcol = _seg_sum(jnp.where(diag, v8[j:j + 1], 0.0), lo_half)
                S = S * w8[j:j + 1] + sa * b8[j:j + 1] + vcol * k8[j:j + 1]
                y = _seg_sum(S * r8[j:j + 1], lo_half)
                rows.append(jnp.sum(jnp.where(diag, y, 0.0), axis=0, keepdims=True))
            y_ref[sl] = jnp.concatenate(rows, axis=0)
            new_states.append(S)
        return tuple(new_states)

    init = tuple(s_ref[q] for q in range(pairs))
    final = lax.fori_loop(0, tb // SUBLANES, body, init)
    for q in range(pairs):
        s_ref[q] = final[q]


def rwkv_scan(r, w, k, v, aa, bb, *, tb=256, pairs=4):
    B, T, W = r.shape
    tb = min(tb, T)
    gw = pairs * LANES
    spec = pl.BlockSpec((1, tb, gw), lambda b, g, i: (b, i, g))
    return pl.pallas_call(
        functools.partial(_rwkv_scan_kernel, pairs=pairs),
        grid=(B, W // gw, T // tb),
        in_specs=[spec] * 6,
        out_specs=spec,
        out_shape=jax.ShapeDtypeStruct((B, T, W), F32),
        scratch_shapes=[pltpu.VMEM((pairs, RWKV_HEAD, LANES), F32)],
        compiler_params=_cparams(("parallel", "parallel", "arbitrary")),
        name="rwkv_scan",
    )(r, w, k, v, aa, bb)


def _rwkv_post_kernel(y_ref, r_ref, k_ref, v_ref, g_ref, vec_ref, o_ref):
    W = RWKV_WIDTH
    ones = _head_sum_matrix(RWKV_HEAD)
    ln_w = vec_ref[0:1, :]
    ln_b = vec_ref[1:2, :]
    r_k = vec_ref[2:3, :]

    def head_sum(x):
        return jnp.concatenate(
            [_dot_exact_rhs(x[:, j * LANES:(j + 1) * LANES], ones) for j in range(W // LANES)],
            axis=1)

    y = y_ref[...]
    mean = head_sum(y) * (1.0 / RWKV_HEAD)
    d = y - mean
    var = head_sum(d * d) * (1.0 / RWKV_HEAD)
    yn = d * lax.rsqrt(var + GN_EPS) * ln_w + ln_b
    bonus = head_sum(r_ref[...] * k_ref[...] * r_k) * v_ref[...]
    o_ref[...] = ((yn + bonus) * g_ref[...]).astype(o_ref.dtype)


def rwkv_post(y, r, k, v, g, ln_w, ln_b, r_k, *, tb=256):
    M, W = y.shape
    tb = min(tb, M)
    vec = jnp.zeros((SUBLANES, W), F32)
    vec = vec.at[0].set(ln_w).at[1].set(ln_b).at[2].set(r_k.reshape(W))
    spec = pl.BlockSpec((tb, W), lambda i: (i, 0))
    return pl.pallas_call(
        _rwkv_post_kernel,
        grid=(M // tb,),
        in_specs=[spec] * 5 + [pl.BlockSpec((SUBLANES, W), lambda i: (0, 0))],
        out_specs=spec,
        out_shape=jax.ShapeDtypeStruct((M, W), BF16),
        compiler_params=_cparams(("parallel",)),
        name="rwkv_post",
    )(y, r, k, v, g, vec)


def _merge_kernel(ym_ref, yr_ref, gm_ref, gr_ref, pm_ref, pr_ref, o_ref):
    m = _dot(ym_ref[...], pm_ref[...])
    r = _dot(yr_ref[...], pr_ref[...])
    o_ref[...] = (jax.nn.sigmoid(gm_ref[...]) * m
                  + jax.nn.sigmoid(gr_ref[...]) * r).astype(o_ref.dtype)


def merge(y_m, y_r, gates, proj_m, proj_r, *, tm=512, tn=512):
    M, K = y_m.shape
    D = proj_m.shape[1]
    tm = min(tm, M)
    return pl.pallas_call(
        _merge_kernel,
        grid=(M // tm, D // tn),
        in_specs=[
            pl.BlockSpec((tm, K), lambda i, j: (i, 0)),
            pl.BlockSpec((tm, K), lambda i, j: (i, 0)),
            pl.BlockSpec((tm, tn), lambda i, j: (i, j)),
            pl.BlockSpec((tm, tn), lambda i, j: (i, D // tn + j)),
            pl.BlockSpec((K, tn), lambda i, j: (0, j)),
            pl.BlockSpec((K, tn), lambda i, j: (0, j)),
        ],
        out_specs=pl.BlockSpec((tm, tn), lambda i, j: (i, j)),
        out_shape=jax.ShapeDtypeStruct((M, D), BF16),
        compiler_params=_cparams(("parallel", "arbitrary")),
        name="merge",
    )(y_m, y_r, gates, gates, proj_m, proj_r)


def _out_proj_kernel(mix_ref, x_ref, w_ref, g_ref, h_ref, xn_ref):
    h = x_ref[...] + _dot(mix_ref[...], w_ref[...])
    h_ref[...] = h
    ms = jnp.mean(h * h, axis=-1, keepdims=True)
    xn_ref[...] = h * lax.rsqrt(ms + RMS_EPS) * g_ref[...]


def out_proj(mixed, x2d, w_out_bf16, gain, *, tm=256):
    M, D = x2d.shape
    tm = min(tm, M)
    spec = pl.BlockSpec((tm, D), lambda i: (i, 0))
    return pl.pallas_call(
        _out_proj_kernel,
        grid=(M // tm,),
        in_specs=[spec, spec, pl.BlockSpec((D, D), lambda i: (0, 0)),
                  pl.BlockSpec((1, D), lambda i: (0, 0))],
        out_specs=[spec, spec],
        out_shape=[jax.ShapeDtypeStruct((M, D), F32)] * 2,
        compiler_params=_cparams(("parallel",)),
        name="out_proj",
    )(mixed, x2d, w_out_bf16, gain.reshape(1, D))


def _matmul_split_kernel(a_ref, whi_ref, wlo_ref, o_ref, hi_ref, lo_ref):
    @pl.when(pl.program_id(1) == 0)
    def _():
        hi, lo = _split2(a_ref[...])
        hi_ref[...] = hi
        lo_ref[...] = lo

    o_ref[...] = (_dot(hi_ref[...], whi_ref[...]) + _dot(lo_ref[...], whi_ref[...])
                  + _dot(hi_ref[...], wlo_ref[...]))


def matmul_split(a, w, *, tm=512, tn=512):
    M, K = a.shape
    N = w.shape[1]
    tm = min(tm, M)
    whi = w.astype(BF16)
    wlo = (w - whi.astype(F32)).astype(BF16)
    return pl.pallas_call(
        _matmul_split_kernel,
        grid=(M // tm, N // tn),
        in_specs=[pl.BlockSpec((tm, K), lambda i, j: (i, 0)),
                  pl.BlockSpec((K, tn), lambda i, j: (0, j)),
                  pl.BlockSpec((K, tn), lambda i, j: (0, j))],
        out_specs=pl.BlockSpec((tm, tn), lambda i, j: (i, j)),
        out_shape=jax.ShapeDtypeStruct((M, N), F32),
        scratch_shapes=[pltpu.VMEM((tm, K), BF16), pltpu.VMEM((tm, K), BF16)],
        compiler_params=_cparams(("parallel", "arbitrary")),
        name="peer_query",
    )(a, whi, wlo)


def _route_kernel(q_ref, khi_ref, klo_ref, idx_ref, gate_ref,
                  s_ref, c1_ref, c2_ref, e1_ref, e2_ref):
    tn = q_ref.shape[0]
    KK = PEER_TOPK
    half = D_KEY // 2
    NEG = -jnp.inf
    lane = lax.broadcasted_iota(jnp.int32, (tn, N_KEYS), 1)
    lane2 = lax.broadcasted_iota(jnp.int32, (tn, KK * KK), 1)
    lane2_hi = lane2 // KK
    lane2_lo = lane2 % KK

    for h in range(PEER_HEADS):
        for p in range(2):
            hp = 2 * h + p
            qhi, qlo = _split2(q_ref[:, hp * half:(hp + 1) * half])
            s_ref[hp] = (_dot(qhi, khi_ref[p]) + _dot(qlo, khi_ref[p])
                         + _dot(qhi, klo_ref[p]))
        for ref in (c1_ref, c2_ref, e1_ref, e2_ref):
            ref[h] = jnp.zeros((tn, KK * KK), F32)

    def sub_topk(kk, carry):
        for h in range(PEER_HEADS):
            for p in range(2):
                hp = 2 * h + p
                s = s_ref[hp]
                m = jnp.max(s, axis=1, keepdims=True)
                pos = jnp.min(jnp.where(s == m, lane, N_KEYS), axis=1, keepdims=True)
                s_ref[hp] = jnp.where(lane == pos, NEG, s)
                posf = pos.astype(F32)
                if p == 0:
                    sel = lane2_hi == kk
                    c1_ref[h] = jnp.where(sel, m, c1_ref[h])
                    e1_ref[h] = jnp.where(sel, posf * float(N_KEYS), e1_ref[h])
                else:
                    sel = lane2_lo == kk
                    c2_ref[h] = jnp.where(sel, m, c2_ref[h])
                    e2_ref[h] = jnp.where(sel, posf, e2_ref[h])
        return carry

    lax.fori_loop(0, KK, sub_topk, 0)

    for h in range(PEER_HEADS):
        c1_ref[h] = c1_ref[h] + c2_ref[h]
        e1_ref[h] = e1_ref[h] + e2_ref[h]

    def cand_topk(kk, carry):
        best, ids, top = carry
        for h in range(PEER_HEADS):
            c = c1_ref[h]
            m = jnp.max(c, axis=1, keepdims=True)
            pos = jnp.min(jnp.where(c == m, lane2, KK * KK), axis=1, keepdims=True)
            hit = lane2 == pos
            eid = jnp.max(jnp.where(hit, e1_ref[h], -1.0), axis=1, keepdims=True)
            c1_ref[h] = jnp.where(hit, NEG, c)
            sel = lane == (h * KK + kk)
            best = jnp.where(sel, m, best)
            ids = jnp.where(sel, eid, ids)
            top = jnp.where((lane // KK == h) & (kk == 0), m, top)
        return best, ids, top

    zero = jnp.zeros((tn, N_KEYS), F32)
    best, ids, top = lax.fori_loop(0, KK, cand_topk, (zero, zero, zero))
    e = jnp.exp(best - top)
    denom = _dot_exact_rhs(e, _head_sum_matrix(KK))
    gate_ref[...] = e / denom
    idx_ref[...] = ids.astype(jnp.int32)


def peer_route(q, sub_keys, *, tn=128):
    M = q.shape[0]
    tn = min(tn, M)
    kt = jnp.swapaxes(sub_keys, 1, 2)
    khi = kt.astype(BF16)
    klo = (kt - khi.astype(F32)).astype(BF16)
    kspec = pl.BlockSpec((2, D_KEY // 2, N_KEYS), lambda i: (0, 0, 0))
    ospec = pl.BlockSpec((tn, N_KEYS), lambda i: (i, 0))
    KK2 = PEER_TOPK * PEER_TOPK
    return pl.pallas_call(
        _route_kernel,
        grid=(M // tn,),
        in_specs=[pl.BlockSpec((tn, PEER_HEADS * D_KEY), lambda i: (i, 0)), kspec, kspec],
        out_specs=[ospec, ospec],
        out_shape=[jax.ShapeDtypeStruct((M, N_KEYS), jnp.int32),
                   jax.ShapeDtypeStruct((M, N_KEYS), F32)],
        scratch_shapes=[
            pltpu.VMEM((2 * PEER_HEADS, tn, N_KEYS), F32),
            pltpu.VMEM((PEER_HEADS, tn, KK2), F32),
            pltpu.VMEM((PEER_HEADS, tn, KK2), F32),
            pltpu.VMEM((PEER_HEADS, tn, KK2), F32),
            pltpu.VMEM((PEER_HEADS, tn, KK2), F32),
        ],
        compiler_params=_cparams(("parallel",)),
        name="peer_route",
    )(q, khi, klo)


def _gelu_exact(x):
    return 0.5 * x * (1.0 + lax.erf(x * (2.0 ** -0.5)))


def _peer_expert_kernel(idx_ref, gate_ref, xn_ref, h_ref, gain_ref, uv_ref, o_ref,
                        buf_ref, acc_ref, sem):
    tb, D = xn_ref.shape
    E = idx_ref.shape[1]
    nd = D // LANES

    def slot_copy(slot):
        return pltpu.make_async_copy(buf_ref.at[slot], buf_ref.at[slot], sem.at[slot])

    def issue(t, slot):
        for e in range(E):
            pltpu.make_async_copy(uv_ref.at[idx_ref[t, e]], buf_ref.at[slot, :, e, :],
                                  sem.at[slot]).start()

    issue(0, 0)
    eye = (lax.broadcasted_iota(jnp.int32, (E, E), 0)
           == lax.broadcasted_iota(jnp.int32, (E, E), 1))

    def body(t8, carry):
        base = pl.multiple_of(t8 * SUBLANES, SUBLANES)
        x8 = xn_ref[pl.ds(base, SUBLANES), :]
        g8 = gate_ref[pl.ds(base, SUBLANES), :]
        rows = []
        for j in range(SUBLANES):
            slot = j % 2
            nxt = base + j + 1

            @pl.when(nxt < tb)
            def _():
                issue(nxt, 1 - slot)

            slot_copy(slot).wait()
            part = jnp.zeros((E, LANES), F32)
            for s in range(nd):
                part = part + buf_ref[slot, s] * x8[j:j + 1, s * LANES:(s + 1) * LANES]
            act = jnp.sum(part, axis=1, keepdims=True)
            g_col = jnp.sum(jnp.where(eye, g8[j:j + 1], 0.0), axis=1, keepdims=True)
            c = g_col * _gelu_exact(act)
            rows.append(jnp.concatenate(
                [jnp.sum(buf_ref[slot, nd + s] * c, axis=0, keepdims=True) for s in range(nd)],
                axis=1))
        acc_ref[pl.ds(base, SUBLANES), :] = jnp.concatenate(rows, axis=0)
        return carry

    lax.fori_loop(0, tb // SUBLANES, body, 0)
    hh = h_ref[...] + acc_ref[...]
    ms = jnp.mean(hh * hh, axis=-1, keepdims=True)
    o_ref[...] = hh * lax.rsqrt(ms + RMS_EPS) * gain_ref[...]


def peer_experts(idx, gate, xn, h, gain, uv, *, tb=32):
    M, D = xn.shape
    E = idx.shape[1]
    tb = min(tb, M)
    spec = pl.BlockSpec((tb, D), lambda i: (i, 0))
    return pl.pallas_call(
        _peer_expert_kernel,
        grid=(M // tb,),
        in_specs=[
            pl.BlockSpec((tb, E), lambda i: (i, 0), memory_space=pltpu.SMEM),
            pl.BlockSpec((tb, E), lambda i: (i, 0)),
            spec, spec,
            pl.BlockSpec((1, D), lambda i: (0, 0)),
            pl.BlockSpec(memory_space=pl.ANY),
        ],
        out_specs=spec,
        out_shape=jax.ShapeDtypeStruct((M, D), F32),
        scratch_shapes=[
            pltpu.VMEM((2, 2 * D // LANES, E, LANES), F32),
            pltpu.VMEM((tb, D), F32),
            pltpu.SemaphoreType.DMA((2,)),
        ],
        compiler_params=_cparams(("arbitrary",)),
        name="peer_experts",
    )(idx, gate, xn, h, gain.reshape(1, D), uv)


def _pack_w_in(w_in):
    D = w_in.shape[0]
    W = RWKV_WIDTH
    o = 0
    qk = w_in[:, o:o + 2 * MLSTM_QK]; o += 2 * MLSTM_QK
    v = w_in[:, o:o + MLSTM_V]; o += MLSTM_V
    og = w_in[:, o:o + MLSTM_V]; o += MLSTM_V
    ifg = w_in[:, o:o + 2 * MLSTM_HEADS]; o += 2 * MLSTM_HEADS
    rkv = w_in[:, o:o + 3 * W]; o += 3 * W
    wl = w_in[:, o:o + W_LORA]; o += W_LORA
    al = w_in[:, o:o + A_LORA]; o += A_LORA
    gl = w_in[:, o:o + G_LORA]; o += G_LORA
    gates = w_in[:, o:]

    def pad(t, n):
        return jnp.pad(t, ((0, 0), (0, n - t.shape[1])))

    w_mlstm = jnp.concatenate([qk, v, og, pad(ifg, LANES)], axis=1)
    w_rwkv = jnp.concatenate([rkv, pad(wl, LORA_PAD), pad(al, LORA_PAD), gl], axis=1)
    assert w_mlstm.shape == (D, MLSTM_PACKED) and w_rwkv.shape == (D, RWKV_PACKED)
    return w_mlstm.astype(BF16), w_rwkv.astype(BF16), gates.astype(BF16)


def kernel(x, norm_mix_gain, w_in, mlstm_conv, mlstm_b_i, mlstm_b_f, rwkv_mu, rwkv_w0, rwkv_w2,
           rwkv_a0, rwkv_a2, rwkv_g2, rwkv_k_k, rwkv_k_a, rwkv_r_k, rwkv_ln_w, rwkv_ln_b,
           proj_mlstm, proj_rwkv, w_out, norm_ffn_gain, peer_w_query, peer_sub_keys,
           peer_u, peer_v, norm_final_gain):
    B, T, D = x.shape
    assert w_in.shape[0] == 1, "the output norm is fused into the single layer's PEER kernel"
    l = 0
    x2d = x.reshape(B * T, D)
    w_mlstm, w_rwkv, w_gates = _pack_w_in(w_in[l])
    p_m = norm_matmul(x2d, norm_mix_gain[l], w_mlstm).reshape(B, T, MLSTM_PACKED)
    p_r = norm_matmul(x2d, norm_mix_gain[l], w_rwkv, tn=512).reshape(B, T, RWKV_PACKED)
    p_g = norm_matmul(x2d, norm_mix_gain[l], w_gates, tn=512)

    y_m = mlstm_branch(p_m, mlstm_conv[l], mlstm_b_i[l], mlstm_b_f[l])
    r, w, k, v, aa, bb, g = rwkv_pre(p_r, rwkv_mu[l], rwkv_w0[l], rwkv_w2[l], rwkv_a0[l],
                                     rwkv_a2[l], rwkv_g2[l], rwkv_k_k[l], rwkv_k_a[l])
    y = rwkv_scan(r, w, k, v, aa, bb)

    def flat(t):
        return t.reshape(B * T, RWKV_WIDTH)

    y_r = rwkv_post(flat(y), flat(r), flat(k), flat(v), flat(g),
                    rwkv_ln_w[l], rwkv_ln_b[l], rwkv_r_k[l])
    mixed = merge(y_m.reshape(B * T, MLSTM_V), y_r, p_g,
                  proj_mlstm[l].astype(BF16), proj_rwkv[l].astype(BF16))
    h2d, xn = out_proj(mixed, x2d, w_out[l].astype(BF16), norm_ffn_gain[l])
    q = matmul_split(xn, peer_w_query[l])
    idx, gate = peer_route(q, peer_sub_keys[l])
    uv = jnp.concatenate([peer_u[l], peer_v[l]], axis=1).reshape(-1, 2 * D // LANES, LANES)
    out = peer_experts(idx, gate, xn, h2d, norm_final_gain, uv)
    return out.reshape(B, T, D)
```

```python
import functools

import jax
import jax.numpy as jnp
from jax import lax
from jax.experimental import pallas as pl
from jax.experimental.pallas import tpu as pltpu

F32 = jnp.float32
BF16 = jnp.bfloat16

LANES = 128
SUBLANES = 8
VMEM_LIMIT_BYTES = 48 * 1024 * 1024

MLSTM_HEADS = 4
MLSTM_DK = 256
MLSTM_CONV = 4
MLSTM_CHUNK = 64
RWKV_HEADS = 16
RWKV_HEAD = 64
RWKV_WIDTH = RWKV_HEADS * RWKV_HEAD
W_LORA = 96
A_LORA = 96
G_LORA = 256
PEER_HEADS = 8
N_KEYS = 128
PEER_TOPK = 16
D_KEY = 256
RMS_EPS = 1e-6
GN_EPS = 64e-5
L2_EPS = 1e-12

PEER_SLOTS = 8
PEER_LOOKAHEAD = 4

MLSTM_QK = MLSTM_HEADS * MLSTM_DK
MLSTM_V = MLSTM_QK
LORA_PAD = 128

COL_QK = 0
COL_V = COL_QK + 2 * MLSTM_QK
COL_O = COL_V + MLSTM_V
COL_IF = COL_O + MLSTM_V
MLSTM_PACKED = COL_IF + LANES
RWKV_PACKED = 3 * RWKV_WIDTH + 2 * LORA_PAD + G_LORA


def _cparams(sem):
    return pltpu.CompilerParams(dimension_semantics=sem, vmem_limit_bytes=VMEM_LIMIT_BYTES)


def _split2(x):
    hi = x.astype(BF16)
    lo = (x - hi.astype(F32)).astype(BF16)
    return hi, lo


def _split3(x):
    hi = x.astype(BF16)
    r1 = x - hi.astype(F32)
    mid = r1.astype(BF16)
    lo = (r1 - mid.astype(F32)).astype(BF16)
    return hi, mid, lo


def _dot(a, b):
    return jnp.dot(a, b, preferred_element_type=F32)


def _dot_exact_rhs(x, ones_bf16):
    hi, mid, lo = _split3(x)
    return _dot(hi, ones_bf16) + _dot(mid, ones_bf16) + _dot(lo, ones_bf16)


def _norm_matmul_kernel(x_ref, g_ref, w_ref, o_ref, xn_ref):
    @pl.when(pl.program_id(1) == 0)
    def _():
        x = x_ref[...]
        ms = jnp.mean(x * x, axis=-1, keepdims=True)
        xn_ref[...] = (x * lax.rsqrt(ms + RMS_EPS) * g_ref[...]).astype(BF16)

    o_ref[...] = _dot(xn_ref[...], w_ref[...])


def norm_matmul(x, gain, w_bf16, *, tm=512, tn=384):
    M, K = x.shape
    N = w_bf16.shape[1]
    tm = min(tm, M)
    return pl.pallas_call(
        _norm_matmul_kernel,
        grid=(M // tm, N // tn),
        in_specs=[
            pl.BlockSpec((tm, K), lambda i, j: (i, 0)),
            pl.BlockSpec((1, K), lambda i, j: (0, 0)),
            pl.BlockSpec((K, tn), lambda i, j: (0, j)),
        ],
        out_specs=pl.BlockSpec((tm, tn), lambda i, j: (i, j)),
        out_shape=jax.ShapeDtypeStruct((M, N), F32),
        scratch_shapes=[pltpu.VMEM((tm, K), BF16)],
        compiler_params=_cparams(("parallel", "arbitrary")),
        name="norm_matmul",
    )(x, gain.reshape(1, K), w_bf16)


def _mlstm_kernel(qk_ref, v_ref, o_ref, if_ref, conv_ref, bias_ref, y_ref,
                  ext_ref, c_ref, n_ref, m_ref):
    L = MLSTM_CHUNK
    DK = MLSTM_DK
    step = pl.program_id(1)

    @pl.when(step == 0)
    def _():
        ext_ref[0:SUBLANES, :] = jnp.zeros((SUBLANES, 2 * MLSTM_QK), F32)
        c_ref[...] = jnp.zeros_like(c_ref)
        n_ref[...] = jnp.zeros_like(n_ref)
        m_ref[...] = jnp.zeros_like(m_ref)

    ext_ref[SUBLANES:SUBLANES + L, :] = qk_ref[0]
    acc = jnp.zeros((L, 2 * MLSTM_QK), F32)
    for j in range(MLSTM_CONV):
        off = SUBLANES - (MLSTM_CONV - 1) + j
        acc = acc + ext_ref[off:off + L, :] * conv_ref[j:j + 1, :]
    ext_ref[0:SUBLANES, :] = qk_ref[0, L - SUBLANES:L, :]
    qk = acc * jax.nn.sigmoid(acc)

    row = lax.broadcasted_iota(jnp.int32, (L, L), 0)
    col = lax.broadcasted_iota(jnp.int32, (L, L), 1)
    causal = col <= row
    eye = col == row

    def to_row(x_col):
        return jnp.sum(jnp.where(eye, x_col, 0.0), axis=0, keepdims=True)

    gates = if_ref[0]
    for h in range(MLSTM_HEADS):
        q = qk[:, h * DK:(h + 1) * DK] * (DK ** -0.5)
        k = qk[:, MLSTM_QK + h * DK:MLSTM_QK + (h + 1) * DK]
        v = v_ref[0, :, h * DK:(h + 1) * DK]
        ig_col = gates[:, h:h + 1] + bias_ref[0:1, h:h + 1]
        lf_col = jax.nn.log_sigmoid(
            gates[:, MLSTM_HEADS + h:MLSTM_HEADS + h + 1]
            + bias_ref[1:2, h:h + 1])
        lf_row = to_row(lf_col)
        ig_row = to_row(ig_col)
        b_col = jnp.sum(jnp.where(causal, lf_row, 0.0), axis=1, keepdims=True)
        b_row = to_row(b_col)
        b_last = b_col[L - 1:L, :]
        m_prev = m_ref[h:h + 1, 0:1]
        C = c_ref[h]
        n_row = n_ref[h:h + 1, :]

        dmat = jnp.where(causal, b_col - b_row + ig_row, -jnp.inf)
        inter = b_col + m_prev
        m_t = jnp.maximum(inter, jnp.max(dmat, axis=1, keepdims=True))
        qb = q.astype(BF16)
        kb = k.astype(BF16)
        vb = v.astype(BF16)
        s = lax.dot_general(qb, kb, (((1,), (1,)), ((), ())),
                            preferred_element_type=F32) * jnp.exp(dmat - m_t)
        w_inter = jnp.exp(inter - m_t)
        num = _dot(s.astype(BF16), vb) + w_inter * _dot(qb, C.astype(BF16))
        qn = jnp.sum(q * n_row, axis=1, keepdims=True)
        den = jnp.sum(s, axis=1, keepdims=True) + w_inter * qn
        hh = num / jnp.maximum(jnp.abs(den), jnp.exp(-m_t))
        o = o_ref[0, :, h * DK:(h + 1) * DK]
        y_ref[0, :, h * DK:(h + 1) * DK] = (jax.nn.sigmoid(o) * hh).astype(y_ref.dtype)

        g_end = b_last - b_col + ig_col
        m_new = jnp.maximum(b_last + m_prev, jnp.max(g_end, axis=0, keepdims=True))
        decay = jnp.exp(b_last + m_prev - m_new)
        ws = jnp.exp(g_end - m_new)
        kw = k * ws
        c_ref[h] = decay * C + lax.dot_general(
            kw.astype(BF16), vb, (((0,), (0,)), ((), ())), preferred_element_type=F32)
        n_ref[h:h + 1, :] = decay * n_row + jnp.sum(kw, axis=0, keepdims=True)
        m_ref[h:h + 1, :] = jnp.broadcast_to(m_new, (1, LANES))


def mlstm_branch(p, conv_w, b_i, b_f):
    B, T, _ = p.shape
    L = MLSTM_CHUNK
    bias = jnp.zeros((SUBLANES, LANES), F32)
    bias = bias.at[0, :MLSTM_HEADS].set(b_i).at[1, :MLSTM_HEADS].set(b_f)
    nqk = 2 * MLSTM_QK
    return pl.pallas_call(
        _mlstm_kernel,
        grid=(B, T // L),
        in_specs=[
            pl.BlockSpec((1, L, nqk), lambda b, c: (b, c, COL_QK // nqk)),
            pl.BlockSpec((1, L, MLSTM_V), lambda b, c: (b, c, COL_V // MLSTM_V)),
            pl.BlockSpec((1, L, MLSTM_V), lambda b, c: (b, c, COL_O // MLSTM_V)),
            pl.BlockSpec((1, L, LANES), lambda b, c: (b, c, COL_IF // LANES)),
            pl.BlockSpec((MLSTM_CONV, nqk), lambda b, c: (0, 0)),
            pl.BlockSpec((SUBLANES, LANES), lambda b, c: (0, 0)),
        ],
        out_specs=pl.BlockSpec((1, L, MLSTM_V), lambda b, c: (b, c, 0)),
        out_shape=jax.ShapeDtypeStruct((B, T, MLSTM_V), BF16),
        scratch_shapes=[
            pltpu.VMEM((SUBLANES + L, nqk), F32),
            pltpu.VMEM((MLSTM_HEADS, MLSTM_DK, MLSTM_DK), F32),
            pltpu.VMEM((SUBLANES, MLSTM_DK), F32),
            pltpu.VMEM((SUBLANES, LANES), F32),
        ],
        compiler_params=_cparams(("parallel", "arbitrary")),
        name="mlstm",
    )(p, p, p, p, conv_w, bias)


def _head_sum_matrix(group):
    r = lax.broadcasted_iota(jnp.int32, (LANES, LANES), 0) // group
    c = lax.broadcasted_iota(jnp.int32, (LANES, LANES), 1) // group
    return jnp.where(r == c, 1.0, 0.0).astype(BF16)


def _rwkv_pre_kernel(p_ref, prev_ref, mu_ref, vec_ref, w2_ref, a2_ref, g2_ref,
                     r_ref, w_ref, k_ref, v_ref, aa_ref, bb_ref, g_ref):
    W = RWKV_WIDTH
    tb = p_ref.shape[1]
    p = p_ref[0]
    last = prev_ref[0, SUBLANES - 1:SUBLANES, :]
    last = jnp.where(pl.program_id(1) == 0, 0.0, last)
    rid = lax.broadcasted_iota(jnp.int32, (tb, 1), 0)
    shifted = jnp.where(rid == 0, last, pltpu.roll(p, 1, 0))
    p = p + (shifted - p) * mu_ref[...]

    r = p[:, 0:W]
    k = p[:, W:2 * W]
    v = p[:, 2 * W:3 * W]
    wl = p[:, 3 * W:3 * W + LORA_PAD]
    al = p[:, 3 * W + LORA_PAD:3 * W + 2 * LORA_PAD]
    gl = p[:, 3 * W + 2 * LORA_PAD:]
    w0 = vec_ref[0:1, :]
    a0 = vec_ref[1:2, :]
    k_k = vec_ref[2:3, :]
    k_a = vec_ref[3:4, :]

    wx = -(w0 + _dot(jnp.tanh(wl).astype(BF16), w2_ref[...]))
    softplus = jnp.maximum(wx, 0.0) + jnp.log1p(jnp.exp(-jnp.abs(wx)))
    w = -softplus - 0.5
    decay = jnp.exp(-jnp.exp(w))
    a = jax.nn.sigmoid(a0 + _dot(al.astype(BF16), a2_ref[...]))
    g = _dot(jax.nn.sigmoid(gl).astype(BF16), g2_ref[...])

    kk = k * k_k
    ones = _head_sum_matrix(RWKV_HEAD)
    sq = kk * kk
    ss = jnp.concatenate(
        [_dot_exact_rhs(sq[:, j * LANES:(j + 1) * LANES], ones) for j in range(W // LANES)],
        axis=1)
    kk = kk / jnp.maximum(jnp.sqrt(ss), L2_EPS)

    r_ref[0] = r
    w_ref[0] = decay
    k_ref[0] = k * (1.0 + (a - 1.0) * k_a)
    v_ref[0] = v
    aa_ref[0] = -kk
    bb_ref[0] = kk * a
    g_ref[0] = g


def rwkv_pre(p, mu, w0, w2, a0, a2, g2, k_k, k_a, *, tb=256):
    B, T, _ = p.shape
    W = RWKV_WIDTH
    tb = min(tb, T)
    mu_p = jnp.zeros((1, RWKV_PACKED), F32)
    mu_p = mu_p.at[0, :3 * W].set(mu[:3 * W])
    mu_p = mu_p.at[0, 3 * W:3 * W + W_LORA].set(mu[3 * W:3 * W + W_LORA])
    mu_p = mu_p.at[0, 3 * W + LORA_PAD:3 * W + LORA_PAD + A_LORA].set(
        mu[3 * W + W_LORA:3 * W + W_LORA + A_LORA])
    mu_p = mu_p.at[0, 3 * W + 2 * LORA_PAD:].set(mu[3 * W + W_LORA + A_LORA:])
    vec = jnp.zeros((SUBLANES, W), F32)
    vec = vec.at[0].set(w0).at[1].set(a0).at[2].set(k_k).at[3].set(k_a)
    w2_p = jnp.zeros((LORA_PAD, W), F32).at[:W_LORA].set(w2).astype(BF16)
    a2_p = jnp.zeros((LORA_PAD, W), F32).at[:A_LORA].set(a2).astype(BF16)
    nprev = tb // SUBLANES
    out = jax.ShapeDtypeStruct((B, T, W), F32)
    ospec = pl.BlockSpec((1, tb, W), lambda b, i: (b, i, 0))
    return pl.pallas_call(
        _rwkv_pre_kernel,
        grid=(B, T // tb),
        in_specs=[
            pl.BlockSpec((1, tb, RWKV_PACKED), lambda b, i: (b, i, 0)),
            pl.BlockSpec((1, SUBLANES, RWKV_PACKED),
                         lambda b, i: (b, jnp.maximum(i * nprev - 1, 0), 0)),
            pl.BlockSpec((1, RWKV_PACKED), lambda b, i: (0, 0)),
            pl.BlockSpec((SUBLANES, W), lambda b, i: (0, 0)),
            pl.BlockSpec((LORA_PAD, W), lambda b, i: (0, 0)),
            pl.BlockSpec((LORA_PAD, W), lambda b, i: (0, 0)),
            pl.BlockSpec((G_LORA, W), lambda b, i: (0, 0)),
        ],
        out_specs=[ospec] * 7,
        out_shape=[out] * 7,
        compiler_params=_cparams(("parallel", "parallel")),
        name="rwkv_pre",
    )(p, p, mu_p, vec, w2_p, a2_p, g2.astype(BF16))


def _seg_sum(x, lo_half):
    s0 = jnp.sum(jnp.where(lo_half, x, 0.0), axis=1, keepdims=True)
    s1 = jnp.sum(jnp.where(lo_half, 0.0, x), axis=1, keepdims=True)
    return jnp.where(lo_half, s0, s1)


def _rwkv_scan_kernel(r_ref, w_ref, k_ref, v_ref, aa_ref, bb_ref, y_ref, s_ref, *, pairs, group):
    N = RWKV_HEAD
    tb = r_ref.shape[1]

    @pl.when(pl.program_id(2) == 0)
    def _():
        s_ref[...] = jnp.zeros_like(s_ref)

    lane = lax.broadcasted_iota(jnp.int32, (N, LANES), 1)
    sub = lax.broadcasted_iota(jnp.int32, (N, LANES), 0)
    lo_half = lane < N
    diag = (lane % N) == sub
    ones = _head_sum_matrix(N)

    def seg_sum_group(xs):
        parts = [_split2(x) for x in xs]
        stacked = jnp.concatenate([p[0] for p in parts] + [p[1] for p in parts], axis=0)
        out = _dot(stacked, ones)
        n = len(xs)
        return [out[q * N:(q + 1) * N] + out[(n + q) * N:(n + q + 1) * N] for q in range(n)]

    def seg_sum_all(xs):
        out = []
        for g in range(0, len(xs), group):
            out += seg_sum_group(xs[g:g + group])
        return out

    def body(t8, states):
        base = pl.multiple_of(t8 * SUBLANES, SUBLANES)
        S = list(states)
        sls = [(0, pl.ds(base, SUBLANES), slice(q * LANES, (q + 1) * LANES)) for q in range(pairs)]
        ops = [[ref[sl] for ref in (aa_ref, bb_ref, w_ref, k_ref, r_ref, v_ref)] for sl in sls]
        rows = [[] for _ in range(pairs)]
        for j in range(SUBLANES):
            row = slice(j, j + 1)
            sa = seg_sum_all([S[q] * ops[q][0][row] for q in range(pairs)])
            vcol = [_seg_sum(jnp.where(diag, ops[q][5][row], 0.0), lo_half) for q in range(pairs)]
            for q in range(pairs):
                a8, b8, w8, k8, r8, v8 = ops[q]
                S[q] = S[q] * w8[row] + sa[q] * b8[row] + vcol[q] * k8[row]
            y = seg_sum_all([S[q] * ops[q][4][row] for q in range(pairs)])
            for q in range(pairs):
                rows[q].append(jnp.sum(jnp.where(diag, y[q], 0.0), axis=0, keepdims=True))
        for q in range(pairs):
            y_ref[sls[q]] = jnp.concatenate(rows[q], axis=0)
        return tuple(S)

    init = tuple(s_ref[q] for q in range(pairs))
    final = lax.fori_loop(0, tb // SUBLANES, body, init)
    for q in range(pairs):
        s_ref[q] = final[q]


def rwkv_scan(r, w, k, v, aa, bb, *, tb=256, pairs=8, group=1):
    B, T, W = r.shape
    tb = min(tb, T)
    gw = pairs * LANES
    spec = pl.BlockSpec((1, tb, gw), lambda b, g, i: (b, i, g))
    return pl.pallas_call(
        functools.partial(_rwkv_scan_kernel, pairs=pairs, group=group),
        grid=(B, W // gw, T // tb),
        in_specs=[spec] * 6,
        out_specs=spec,
        out_shape=jax.ShapeDtypeStruct((B, T, W), F32),
        scratch_shapes=[pltpu.VMEM((pairs, RWKV_HEAD, LANES), F32)],
        compiler_params=_cparams(("parallel", "parallel", "arbitrary")),
        name="rwkv_scan",
    )(r, w, k, v, aa, bb)


def _rwkv_post_kernel(y_ref, r_ref, k_ref, v_ref, g_ref, vec_ref, o_ref):
    W = RWKV_WIDTH
    ones = _head_sum_matrix(RWKV_HEAD)
    ln_w = vec_ref[0:1, :]
    ln_b = vec_ref[1:2, :]
    r_k = vec_ref[2:3, :]

    def head_sum(x):
        return jnp.concatenate(
            [_dot_exact_rhs(x[:, j * LANES:(j + 1) * LANES], ones) for j in range(W // LANES)],
            axis=1)

    y = y_ref[...]
    mean = head_sum(y) * (1.0 / RWKV_HEAD)
    d = y - mean
    var = head_sum(d * d) * (1.0 / RWKV_HEAD)
    yn = d * lax.rsqrt(var + GN_EPS) * ln_w + ln_b
    bonus = head_sum(r_ref[...] * k_ref[...] * r_k) * v_ref[...]
    o_ref[...] = ((yn + bonus) * g_ref[...]).astype(o_ref.dtype)


def rwkv_post(y, r, k, v, g, ln_w, ln_b, r_k, *, tb=256):
    M, W = y.shape
    tb = min(tb, M)
    vec = jnp.zeros((SUBLANES, W), F32)
    vec = vec.at[0].set(ln_w).at[1].set(ln_b).at[2].set(r_k.reshape(W))
    spec = pl.BlockSpec((tb, W), lambda i: (i, 0))
    return pl.pallas_call(
        _rwkv_post_kernel,
        grid=(M // tb,),
        in_specs=[spec] * 5 + [pl.BlockSpec((SUBLANES, W), lambda i: (0, 0))],
        out_specs=spec,
        out_shape=jax.ShapeDtypeStruct((M, W), BF16),
        compiler_params=_cparams(("parallel",)),
        name="rwkv_post",
    )(y, r, k, v, g, vec)


def _merge_kernel(ym_ref, yr_ref, gm_ref, gr_ref, pm_ref, pr_ref, o_ref):
    m = _dot(ym_ref[...], pm_ref[...])
    r = _dot(yr_ref[...], pr_ref[...])
    o_ref[...] = (jax.nn.sigmoid(gm_ref[...]) * m
                  + jax.nn.sigmoid(gr_ref[...]) * r).astype(o_ref.dtype)


def merge(y_m, y_r, gates, proj_m, proj_r, *, tm=512, tn=512):
    M, K = y_m.shape
    D = proj_m.shape[1]
    tm = min(tm, M)
    return pl.pallas_call(
        _merge_kernel,
        grid=(M // tm, D // tn),
        in_specs=[
            pl.BlockSpec((tm, K), lambda i, j: (i, 0)),
            pl.BlockSpec((tm, K), lambda i, j: (i, 0)),
            pl.BlockSpec((tm, tn), lambda i, j: (i, j)),
            pl.BlockSpec((tm, tn), lambda i, j: (i, D // tn + j)),
            pl.BlockSpec((K, tn), lambda i, j: (0, j)),
            pl.BlockSpec((K, tn), lambda i, j: (0, j)),
        ],
        out_specs=pl.BlockSpec((tm, tn), lambda i, j: (i, j)),
        out_shape=jax.ShapeDtypeStruct((M, D), BF16),
        compiler_params=_cparams(("parallel", "arbitrary")),
        name="merge",
    )(y_m, y_r, gates, gates, proj_m, proj_r)


def _out_proj_kernel(mix_ref, x_ref, w_ref, g_ref, h_ref, xn_ref):
    h = x_ref[...] + _dot(mix_ref[...], w_ref[...])
    h_ref[...] = h
    ms = jnp.mean(h * h, axis=-1, keepdims=True)
    xn_ref[...] = h * lax.rsqrt(ms + RMS_EPS) * g_ref[...]


def out_proj(mixed, x2d, w_out_bf16, gain, *, tm=256):
    M, D = x2d.shape
    tm = min(tm, M)
    spec = pl.BlockSpec((tm, D), lambda i: (i, 0))
    return pl.pallas_call(
        _out_proj_kernel,
        grid=(M // tm,),
        in_specs=[spec, spec, pl.BlockSpec((D, D), lambda i: (0, 0)),
                  pl.BlockSpec((1, D), lambda i: (0, 0))],
        out_specs=[spec, spec],
        out_shape=[jax.ShapeDtypeStruct((M, D), F32)] * 2,
        compiler_params=_cparams(("parallel",)),
        name="out_proj",
    )(mixed, x2d, w_out_bf16, gain.reshape(1, D))


def _matmul_split_kernel(a_ref, whi_ref, wlo_ref, o_ref, hi_ref, lo_ref):
    @pl.when(pl.program_id(1) == 0)
    def _():
        hi, lo = _split2(a_ref[...])
        hi_ref[...] = hi
        lo_ref[...] = lo

    o_ref[...] = (_dot(hi_ref[...], whi_ref[...]) + _dot(lo_ref[...], whi_ref[...])
                  + _dot(hi_ref[...], wlo_ref[...]))


def matmul_split(a, w, *, tm=512, tn=512):
    M, K = a.shape
    N = w.shape[1]
    tm = min(tm, M)
    whi = w.astype(BF16)
    wlo = (w - whi.astype(F32)).astype(BF16)
    return pl.pallas_call(
        _matmul_split_kernel,
        grid=(M // tm, N // tn),
        in_specs=[pl.BlockSpec((tm, K), lambda i, j: (i, 0)),
                  pl.BlockSpec((K, tn), lambda i, j: (0, j)),
                  pl.BlockSpec((K, tn), lambda i, j: (0, j))],
        out_specs=pl.BlockSpec((tm, tn), lambda i, j: (i, j)),
        out_shape=jax.ShapeDtypeStruct((M, N), F32),
        scratch_shapes=[pltpu.VMEM((tm, K), BF16), pltpu.VMEM((tm, K), BF16)],
        compiler_params=_cparams(("parallel", "arbitrary")),
        name="peer_query",
    )(a, whi, wlo)


def _route_kernel(q_ref, khi_ref, klo_ref, idx_ref, gate_ref, s_ref, t_ref, c_ref, e_ref):
    tn = q_ref.shape[0]
    KK = PEER_TOPK
    half = D_KEY // 2
    NEG = -jnp.inf
    GROUP = 8
    lane_i = lax.broadcasted_iota(jnp.int32, (tn, N_KEYS), 1)
    lane = lane_i.astype(F32)
    lane2 = lax.broadcasted_iota(jnp.int32, (tn, KK * KK), 1).astype(F32)

    for hp in range(2 * PEER_HEADS):
        qhi, qlo = _split2(q_ref[:, hp * half:(hp + 1) * half])
        p = hp % 2
        s_ref[hp] = _dot(qhi, khi_ref[p]) + _dot(qlo, khi_ref[p]) + _dot(qhi, klo_ref[p])
    t_ref[...] = jnp.zeros_like(t_ref)

    def sub_topk(kk, carry):
        ms, ps = [], []
        for g in range(0, 2 * PEER_HEADS, GROUP):
            ss = [s_ref[hp] for hp in range(g, g + GROUP)]
            mg = [jnp.max(s, axis=1, keepdims=True) for s in ss]
            pg = [jnp.min(jnp.where(s == m, lane, float(N_KEYS)), axis=1, keepdims=True)
                  for s, m in zip(ss, mg)]
            for i, hp in enumerate(range(g, g + GROUP)):
                s_ref[hp] = jnp.where(lane == pg[i], NEG, ss[i])
            ms += mg
            ps += pg
        for p in range(2):
            vals = t_ref[2 * p]
            poss = t_ref[2 * p + 1]
            for h in range(PEER_HEADS):
                sel = lane_i == (h * KK + kk)
                vals = jnp.where(sel, ms[2 * h + p], vals)
                poss = jnp.where(sel, ps[2 * h + p], poss)
            t_ref[2 * p] = vals
            t_ref[2 * p + 1] = poss
        return carry

    lax.fori_loop(0, KK, sub_topk, 0)

    r = lax.broadcasted_iota(jnp.int32, (N_KEYS, KK * KK), 0)
    c = lax.broadcasted_iota(jnp.int32, (N_KEYS, KK * KK), 1)
    v1, i1, v2, i2 = t_ref[0], t_ref[1].astype(BF16), t_ref[2], t_ref[3].astype(BF16)
    for h in range(PEER_HEADS):
        sel1 = jnp.where(r == h * KK + c // KK, 1.0, 0.0).astype(BF16)
        sel2 = jnp.where(r == h * KK + c % KK, 1.0, 0.0).astype(BF16)
        c_ref[h] = _dot_exact_rhs(v1, sel1) + _dot_exact_rhs(v2, sel2)
        e_ref[h] = _dot(i1, sel1) * float(N_KEYS) + _dot(i2, sel2)

    def cand_topk(kk, carry):
        best, ids, top = carry
        cs = [c_ref[h] for h in range(PEER_HEADS)]
        ms = [jnp.max(cc, axis=1, keepdims=True) for cc in cs]
        ps = [jnp.min(jnp.where(cc == m, lane2, float(KK * KK)), axis=1, keepdims=True)
              for cc, m in zip(cs, ms)]
        hits = [lane2 == pp for pp in ps]
        eids = [jnp.max(jnp.where(hit, e_ref[h], -1.0), axis=1, keepdims=True)
                for h, hit in enumerate(hits)]
        for h in range(PEER_HEADS):
            c_ref[h] = jnp.where(hits[h], NEG, cs[h])
            sel = lane_i == (h * KK + kk)
            best = jnp.where(sel, ms[h], best)
            ids = jnp.where(sel, eids[h], ids)
        return best, ids, jnp.where(kk == 0, best, top)

    zero = jnp.zeros((tn, N_KEYS), F32)
    best, ids, top = lax.fori_loop(0, KK, cand_topk, (zero, zero, zero))
    head_max = _dot_exact_rhs(jnp.where(lane_i % KK == 0, top, 0.0), _head_sum_matrix(KK))
    e = jnp.exp(best - head_max)
    denom = _dot_exact_rhs(e, _head_sum_matrix(KK))
    gate_ref[...] = e / denom
    idx_ref[...] = ids.astype(jnp.int32)


def peer_route(q, sub_keys, *, tn=128):
    M = q.shape[0]
    tn = min(tn, M)
    kt = jnp.swapaxes(sub_keys, 1, 2)
    khi = kt.astype(BF16)
    klo = (kt - khi.astype(F32)).astype(BF16)
    kspec = pl.BlockSpec((2, D_KEY // 2, N_KEYS), lambda i: (0, 0, 0))
    ospec = pl.BlockSpec((tn, N_KEYS), lambda i: (i, 0))
    KK2 = PEER_TOPK * PEER_TOPK
    return pl.pallas_call(
        _route_kernel,
        grid=(M // tn,),
        in_specs=[pl.BlockSpec((tn, PEER_HEADS * D_KEY), lambda i: (i, 0)), kspec, kspec],
        out_specs=[ospec, ospec],
        out_shape=[jax.ShapeDtypeStruct((M, N_KEYS), jnp.int32),
                   jax.ShapeDtypeStruct((M, N_KEYS), F32)],
        scratch_shapes=[
            pltpu.VMEM((2 * PEER_HEADS, tn, N_KEYS), F32),
            pltpu.VMEM((4, tn, N_KEYS), F32),
            pltpu.VMEM((PEER_HEADS, tn, KK2), F32),
            pltpu.VMEM((PEER_HEADS, tn, KK2), F32),
        ],
        compiler_params=_cparams(("parallel",)),
        name="peer_route",
    )(q, khi, klo)


def _gelu_exact(x):
    return 0.5 * x * (1.0 + lax.erf(x * (2.0 ** -0.5)))


def _peer_expert_kernel(idx_ref, gate_ref, xn_ref, h_ref, gain_ref, uv_ref, o_ref,
                        buf_ref, acc_ref, sem):
    tb, D = xn_ref.shape
    E = idx_ref.shape[2]
    nd = D // LANES
    step = pl.program_id(0)

    def slot_copy(slot):
        return pltpu.make_async_copy(buf_ref.at[slot], buf_ref.at[slot], sem.at[slot])

    def issue(t, slot):
        for e in range(E):
            pltpu.make_async_copy(uv_ref.at[idx_ref[0, t, e]], buf_ref.at[slot, :, e, :],
                                  sem.at[slot]).start()

    @pl.when(step == 0)
    def _():
        for t in range(PEER_LOOKAHEAD):
            issue(t, t)

    eye = (lax.broadcasted_iota(jnp.int32, (E, E), 0)
           == lax.broadcasted_iota(jnp.int32, (E, E), 1))

    def body(t8, carry):
        base = pl.multiple_of(t8 * SUBLANES, SUBLANES)
        x8 = xn_ref[pl.ds(base, SUBLANES), :]
        g8 = gate_ref[pl.ds(base, SUBLANES), :]
        rows = []
        for j in range(SUBLANES):
            slot = j % PEER_SLOTS
            issue(base + j + PEER_LOOKAHEAD, (j + PEER_LOOKAHEAD) % PEER_SLOTS)
            slot_copy(slot).wait()
            part = jnp.zeros((E, LANES), F32)
            for s in range(nd):
                part = part + buf_ref[slot, s] * x8[j:j + 1, s * LANES:(s + 1) * LANES]
            act = jnp.sum(part, axis=1, keepdims=True)
            g_col = jnp.sum(jnp.where(eye, g8[j:j + 1], 0.0), axis=1, keepdims=True)
            c = g_col * _gelu_exact(act)
            rows.append(jnp.concatenate(
                [jnp.sum(buf_ref[slot, nd + s] * c, axis=0, keepdims=True) for s in range(nd)],
                axis=1))
        acc_ref[pl.ds(base, SUBLANES), :] = jnp.concatenate(rows, axis=0)
        return carry

    lax.fori_loop(0, tb // SUBLANES, body, 0)

    @pl.when(step == pl.num_programs(0) - 1)
    def _():
        for t in range(PEER_LOOKAHEAD):
            slot_copy(t % PEER_SLOTS).wait()

    hh = h_ref[...] + acc_ref[...]
    ms = jnp.mean(hh * hh, axis=-1, keepdims=True)
    o_ref[...] = hh * lax.rsqrt(ms + RMS_EPS) * gain_ref[...]


def peer_experts(idx, gate, xn, h, gain, uv, *, tb=32):
    M, D = xn.shape
    E = idx.shape[1]
    tb = min(tb, M)
    assert tb % PEER_SLOTS == 0 and M % tb == 0
    nb = M // tb
    idx3 = idx.reshape(nb, tb, E)
    ahead = jnp.concatenate([idx3[1:, :PEER_SLOTS], idx3[-1:, :PEER_SLOTS]], axis=0)
    idx_ext = jnp.concatenate([idx3, ahead], axis=1)
    spec = pl.BlockSpec((tb, D), lambda i: (i, 0))
    return pl.pallas_call(
        _peer_expert_kernel,
        grid=(nb,),
        in_specs=[
            pl.BlockSpec((1, tb + PEER_SLOTS, E), lambda i: (i, 0, 0), memory_space=pltpu.SMEM),
            pl.BlockSpec((tb, E), lambda i: (i, 0)),
            spec, spec,
            pl.BlockSpec((1, D), lambda i: (0, 0)),
            pl.BlockSpec(memory_space=pl.ANY),
        ],
        out_specs=spec,
        out_shape=jax.ShapeDtypeStruct((M, D), F32),
        scratch_shapes=[
            pltpu.VMEM((PEER_SLOTS, 2 * D // LANES, E, LANES), F32),
            pltpu.VMEM((tb, D), F32),
            pltpu.SemaphoreType.DMA((PEER_SLOTS,)),
        ],
        compiler_params=_cparams(("arbitrary",)),
        name="peer_experts",
    )(idx_ext, gate, xn, h, gain.reshape(1, D), uv)


def _pack_w_in(w_in):
    D = w_in.shape[0]
    W = RWKV_WIDTH
    o = 0
    qk = w_in[:, o:o + 2 * MLSTM_QK]; o += 2 * MLSTM_QK
    v = w_in[:, o:o + MLSTM_V]; o += MLSTM_V
    og = w_in[:, o:o + MLSTM_V]; o += MLSTM_V
    ifg = w_in[:, o:o + 2 * MLSTM_HEADS]; o += 2 * MLSTM_HEADS
    rkv = w_in[:, o:o + 3 * W]; o += 3 * W
    wl = w_in[:, o:o + W_LORA]; o += W_LORA
    al = w_in[:, o:o + A_LORA]; o += A_LORA
    gl = w_in[:, o:o + G_LORA]; o += G_LORA
    gates = w_in[:, o:]

    def pad(t, n):
        return jnp.pad(t, ((0, 0), (0, n - t.shape[1])))

    w_mlstm = jnp.concatenate([qk, v, og, pad(ifg, LANES)], axis=1)
    w_rwkv = jnp.concatenate([rkv, pad(wl, LORA_PAD), pad(al, LORA_PAD), gl], axis=1)
    assert w_mlstm.shape == (D, MLSTM_PACKED) and w_rwkv.shape == (D, RWKV_PACKED)
    return w_mlstm.astype(BF16), w_rwkv.astype(BF16), gates.astype(BF16)


def kernel(x, norm_mix_gain, w_in, mlstm_conv, mlstm_b_i, mlstm_b_f, rwkv_mu, rwkv_w0, rwkv_w2,
           rwkv_a0, rwkv_a2, rwkv_g2, rwkv_k_k, rwkv_k_a, rwkv_r_k, rwkv_ln_w, rwkv_ln_b,
           proj_mlstm, proj_rwkv, w_out, norm_ffn_gain, peer_w_query, peer_sub_keys,
           peer_u, peer_v, norm_final_gain):
    B, T, D = x.shape
    assert w_in.shape[0] == 1, "the output norm is fused into the single layer's PEER kernel"
    l = 0
    x2d = x.reshape(B * T, D)
    w_mlstm, w_rwkv, w_gates = _pack_w_in(w_in[l])
    p_m = norm_matmul(x2d, norm_mix_gain[l], w_mlstm).reshape(B, T, MLSTM_PACKED)
    p_r = norm_matmul(x2d, norm_mix_gain[l], w_rwkv, tn=512).reshape(B, T, RWKV_PACKED)
    p_g = norm_matmul(x2d, norm_mix_gain[l], w_gates, tn=512)

    y_m = mlstm_branch(p_m, mlstm_conv[l], mlstm_b_i[l], mlstm_b_f[l])
    r, w, k, v, aa, bb, g = rwkv_pre(p_r, rwkv_mu[l], rwkv_w0[l], rwkv_w2[l], rwkv_a0[l],
                                     rwkv_a2[l], rwkv_g2[l], rwkv_k_k[l], rwkv_k_a[l])
    y = rwkv_scan(r, w, k, v, aa, bb)

    def flat(t):
        return t.reshape(B * T, RWKV_WIDTH)

    y_r = rwkv_post(flat(y), flat(r), flat(k), flat(v), flat(g),
                    rwkv_ln_w[l], rwkv_ln_b[l], rwkv_r_k[l])
    mixed = merge(y_m.reshape(B * T, MLSTM_V), y_r, p_g,
                  proj_mlstm[l].astype(BF16), proj_rwkv[l].astype(BF16))
    h2d, xn = out_proj(mixed, x2d, w_out[l].astype(BF16), norm_ffn_gain[l])
    q = matmul_split(xn, peer_w_query[l])
    idx, gate = peer_route(q, peer_sub_keys[l])
    uv = jnp.concatenate([peer_u[l], peer_v[l]], axis=1).reshape(-1, 2 * D // LANES, LANES)
    out = peer_experts(idx, gate, xn, h2d, norm_final_gain, uv)
    return out.reshape(B, T, D)
```

```python
import functools

import jax
import jax.numpy as jnp
from jax import lax
from jax.experimental import pallas as pl
from jax.experimental.pallas import tpu as pltpu

F32 = jnp.float32
BF16 = jnp.bfloat16

LANES = 128
SUBLANES = 8
VMEM_LIMIT_BYTES = 48 * 1024 * 1024

MLSTM_HEADS = 4
MLSTM_DK = 256
MLSTM_CONV = 4
MLSTM_CHUNK = 64
RWKV_HEADS = 16
RWKV_HEAD = 64
RWKV_WIDTH = RWKV_HEADS * RWKV_HEAD
W_LORA = 96
A_LORA = 96
G_LORA = 256
PEER_HEADS = 8
N_KEYS = 128
PEER_TOPK = 16
D_KEY = 256
RMS_EPS = 1e-6
GN_EPS = 64e-5
L2_EPS = 1e-12

PEER_SLOTS = 8
PEER_LOOKAHEAD = 4

MLSTM_QK = MLSTM_HEADS * MLSTM_DK
MLSTM_V = MLSTM_QK
LORA_PAD = 128

COL_QK = 0
COL_V = COL_QK + 2 * MLSTM_QK
COL_O = COL_V + MLSTM_V
COL_IF = COL_O + MLSTM_V
MLSTM_PACKED = COL_IF + LANES
RWKV_PACKED = 3 * RWKV_WIDTH + 2 * LORA_PAD + G_LORA


def _cparams(sem):
    return pltpu.CompilerParams(dimension_semantics=sem, vmem_limit_bytes=VMEM_LIMIT_BYTES)


def _split2(x):
    hi = x.astype(BF16)
    lo = (x - hi.astype(F32)).astype(BF16)
    return hi, lo


def _split3(x):
    hi = x.astype(BF16)
    r1 = x - hi.astype(F32)
    mid = r1.astype(BF16)
    lo = (r1 - mid.astype(F32)).astype(BF16)
    return hi, mid, lo


def _dot(a, b):
    return jnp.dot(a, b, preferred_element_type=F32)


def _dot_exact_rhs(x, ones_bf16):
    hi, mid, lo = _split3(x)
    return _dot(hi, ones_bf16) + _dot(mid, ones_bf16) + _dot(lo, ones_bf16)


def _norm_matmul_kernel(x_ref, g_ref, w_ref, o_ref, xn_ref):
    @pl.when(pl.program_id(1) == 0)
    def _():
        x = x_ref[...]
        ms = jnp.mean(x * x, axis=-1, keepdims=True)
        xn_ref[...] = (x * lax.rsqrt(ms + RMS_EPS) * g_ref[...]).astype(BF16)

    o_ref[...] = _dot(xn_ref[...], w_ref[...])


def norm_matmul(x, gain, w_bf16, *, tm=512, tn=384):
    M, K = x.shape
    N = w_bf16.shape[1]
    tm = min(tm, M)
    return pl.pallas_call(
        _norm_matmul_kernel,
        grid=(M // tm, N // tn),
        in_specs=[
            pl.BlockSpec((tm, K), lambda i, j: (i, 0)),
            pl.BlockSpec((1, K), lambda i, j: (0, 0)),
            pl.BlockSpec((K, tn), lambda i, j: (0, j)),
        ],
        out_specs=pl.BlockSpec((tm, tn), lambda i, j: (i, j)),
        out_shape=jax.ShapeDtypeStruct((M, N), F32),
        scratch_shapes=[pltpu.VMEM((tm, K), BF16)],
        compiler_params=_cparams(("parallel", "arbitrary")),
        name="norm_matmul",
    )(x, gain.reshape(1, K), w_bf16)


def _mlstm_kernel(qk_ref, v_ref, o_ref, if_ref, conv_ref, bias_ref, y_ref,
                  ext_ref, c_ref, n_ref, m_ref):
    L = MLSTM_CHUNK
    DK = MLSTM_DK
    step = pl.program_id(1)

    @pl.when(step == 0)
    def _():
        ext_ref[0:SUBLANES, :] = jnp.zeros((SUBLANES, 2 * MLSTM_QK), F32)
        c_ref[...] = jnp.zeros_like(c_ref)
        n_ref[...] = jnp.zeros_like(n_ref)
        m_ref[...] = jnp.zeros_like(m_ref)

    ext_ref[SUBLANES:SUBLANES + L, :] = qk_ref[0]
    acc = jnp.zeros((L, 2 * MLSTM_QK), F32)
    for j in range(MLSTM_CONV):
        off = SUBLANES - (MLSTM_CONV - 1) + j
        acc = acc + ext_ref[off:off + L, :] * conv_ref[j:j + 1, :]
    ext_ref[0:SUBLANES, :] = qk_ref[0, L - SUBLANES:L, :]
    qk = acc * jax.nn.sigmoid(acc)

    row = lax.broadcasted_iota(jnp.int32, (L, L), 0)
    col = lax.broadcasted_iota(jnp.int32, (L, L), 1)
    causal = col <= row
    eye = col == row

    def to_row(x_col):
        return jnp.sum(jnp.where(eye, x_col, 0.0), axis=0, keepdims=True)

    gates = if_ref[0]
    for h in range(MLSTM_HEADS):
        q = qk[:, h * DK:(h + 1) * DK] * (DK ** -0.5)
        k = qk[:, MLSTM_QK + h * DK:MLSTM_QK + (h + 1) * DK]
        v = v_ref[0, :, h * DK:(h + 1) * DK]
        ig_col = gates[:, h:h + 1] + bias_ref[0:1, h:h + 1]
        lf_col = jax.nn.log_sigmoid(
            gates[:, MLSTM_HEADS + h:MLSTM_HEADS + h + 1]
            + bias_ref[1:2, h:h + 1])
        lf_row = to_row(lf_col)
        ig_row = to_row(ig_col)
        b_col = jnp.sum(jnp.where(causal, lf_row, 0.0), axis=1, keepdims=True)
        b_row = to_row(b_col)
        b_last = b_col[L - 1:L, :]
        m_prev = m_ref[h:h + 1, 0:1]
        C = c_ref[h]
        n_row = n_ref[h:h + 1, :]

        dmat = jnp.where(causal, b_col - b_row + ig_row, -jnp.inf)
        inter = b_col + m_prev
        m_t = jnp.maximum(inter, jnp.max(dmat, axis=1, keepdims=True))
        qb = q.astype(BF16)
        kb = k.astype(BF16)
        vb = v.astype(BF16)
        s = lax.dot_general(qb, kb, (((1,), (1,)), ((), ())),
                            preferred_element_type=F32) * jnp.exp(dmat - m_t)
        w_inter = jnp.exp(inter - m_t)
        num = _dot(s.astype(BF16), vb) + w_inter * _dot(qb, C.astype(BF16))
        qn = jnp.sum(q * n_row, axis=1, keepdims=True)
        den = jnp.sum(s, axis=1, keepdims=True) + w_inter * qn
        hh = num / jnp.maximum(jnp.abs(den), jnp.exp(-m_t))
        o = o_ref[0, :, h * DK:(h + 1) * DK]
        y_ref[0, :, h * DK:(h + 1) * DK] = (jax.nn.sigmoid(o) * hh).astype(y_ref.dtype)

        g_end = b_last - b_col + ig_col
        m_new = jnp.maximum(b_last + m_prev, jnp.max(g_end, axis=0, keepdims=True))
        decay = jnp.exp(b_last + m_prev - m_new)
        ws = jnp.exp(g_end - m_new)
        kw = k * ws
        c_ref[h] = decay * C + lax.dot_general(
            kw.astype(BF16), vb, (((0,), (0,)), ((), ())), preferred_element_type=F32)
        n_ref[h:h + 1, :] = decay * n_row + jnp.sum(kw, axis=0, keepdims=True)
        m_ref[h:h + 1, :] = jnp.broadcast_to(m_new, (1, LANES))


def mlstm_branch(p, conv_w, b_i, b_f):
    B, T, _ = p.shape
    L = MLSTM_CHUNK
    bias = jnp.zeros((SUBLANES, LANES), F32)
    bias = bias.at[0, :MLSTM_HEADS].set(b_i).at[1, :MLSTM_HEADS].set(b_f)
    nqk = 2 * MLSTM_QK
    return pl.pallas_call(
        _mlstm_kernel,
        grid=(B, T // L),
        in_specs=[
            pl.BlockSpec((1, L, nqk), lambda b, c: (b, c, COL_QK // nqk)),
            pl.BlockSpec((1, L, MLSTM_V), lambda b, c: (b, c, COL_V // MLSTM_V)),
            pl.BlockSpec((1, L, MLSTM_V), lambda b, c: (b, c, COL_O // MLSTM_V)),
            pl.BlockSpec((1, L, LANES), lambda b, c: (b, c, COL_IF // LANES)),
            pl.BlockSpec((MLSTM_CONV, nqk), lambda b, c: (0, 0)),
            pl.BlockSpec((SUBLANES, LANES), lambda b, c: (0, 0)),
        ],
        out_specs=pl.BlockSpec((1, L, MLSTM_V), lambda b, c: (b, c, 0)),
        out_shape=jax.ShapeDtypeStruct((B, T, MLSTM_V), BF16),
        scratch_shapes=[
            pltpu.VMEM((SUBLANES + L, nqk), F32),
            pltpu.VMEM((MLSTM_HEADS, MLSTM_DK, MLSTM_DK), F32),
            pltpu.VMEM((SUBLANES, MLSTM_DK), F32),
            pltpu.VMEM((SUBLANES, LANES), F32),
        ],
        compiler_params=_cparams(("parallel", "arbitrary")),
        name="mlstm",
    )(p, p, p, p, conv_w, bias)


def _head_sum_matrix(group):
    r = lax.broadcasted_iota(jnp.int32, (LANES, LANES), 0) // group
    c = lax.broadcasted_iota(jnp.int32, (LANES, LANES), 1) // group
    return jnp.where(r == c, 1.0, 0.0).astype(BF16)


def _rwkv_pre_kernel(p_ref, prev_ref, mu_ref, vec_ref, w2_ref, a2_ref, g2_ref,
                     r_ref, w_ref, k_ref, v_ref, aa_ref, bb_ref, g_ref):
    W = RWKV_WIDTH
    tb = p_ref.shape[1]
    p = p_ref[0]
    last = prev_ref[0, SUBLANES - 1:SUBLANES, :]
    last = jnp.where(pl.program_id(1) == 0, 0.0, last)
    rid = lax.broadcasted_iota(jnp.int32, (tb, 1), 0)
    shifted = jnp.where(rid == 0, last, pltpu.roll(p, 1, 0))
    p = p + (shifted - p) * mu_ref[...]

    r = p[:, 0:W]
    k = p[:, W:2 * W]
    v = p[:, 2 * W:3 * W]
    wl = p[:, 3 * W:3 * W + LORA_PAD]
    al = p[:, 3 * W + LORA_PAD:3 * W + 2 * LORA_PAD]
    gl = p[:, 3 * W + 2 * LORA_PAD:]
    w0 = vec_ref[0:1, :]
    a0 = vec_ref[1:2, :]
    k_k = vec_ref[2:3, :]
    k_a = vec_ref[3:4, :]

    wx = -(w0 + _dot(jnp.tanh(wl).astype(BF16), w2_ref[...]))
    softplus = jnp.maximum(wx, 0.0) + jnp.log1p(jnp.exp(-jnp.abs(wx)))
    w = -softplus - 0.5
    decay = jnp.exp(-jnp.exp(w))
    a = jax.nn.sigmoid(a0 + _dot(al.astype(BF16), a2_ref[...]))
    g = _dot(jax.nn.sigmoid(gl).astype(BF16), g2_ref[...])

    kk = k * k_k
    ones = _head_sum_matrix(RWKV_HEAD)
    sq = kk * kk
    ss = jnp.concatenate(
        [_dot_exact_rhs(sq[:, j * LANES:(j + 1) * LANES], ones) for j in range(W // LANES)],
        axis=1)
    kk = kk / jnp.maximum(jnp.sqrt(ss), L2_EPS)

    r_ref[0] = r
    w_ref[0] = decay
    k_ref[0] = k * (1.0 + (a - 1.0) * k_a)
    v_ref[0] = v
    aa_ref[0] = -kk
    bb_ref[0] = kk * a
    g_ref[0] = g


def rwkv_pre(p, mu, w0, w2, a0, a2, g2, k_k, k_a, *, tb=256):
    B, T, _ = p.shape
    W = RWKV_WIDTH
    tb = min(tb, T)
    mu_p = jnp.zeros((1, RWKV_PACKED), F32)
    mu_p = mu_p.at[0, :3 * W].set(mu[:3 * W])
    mu_p = mu_p.at[0, 3 * W:3 * W + W_LORA].set(mu[3 * W:3 * W + W_LORA])
    mu_p = mu_p.at[0, 3 * W + LORA_PAD:3 * W + LORA_PAD + A_LORA].set(
        mu[3 * W + W_LORA:3 * W + W_LORA + A_LORA])
    mu_p = mu_p.at[0, 3 * W + 2 * LORA_PAD:].set(mu[3 * W + W_LORA + A_LORA:])
    vec = jnp.zeros((SUBLANES, W), F32)
    vec = vec.at[0].set(w0).at[1].set(a0).at[2].set(k_k).at[3].set(k_a)
    w2_p = jnp.zeros((LORA_PAD, W), F32).at[:W_LORA].set(w2).astype(BF16)
    a2_p = jnp.zeros((LORA_PAD, W), F32).at[:A_LORA].set(a2).astype(BF16)
    nprev = tb // SUBLANES
    out = jax.ShapeDtypeStruct((B, T, W), F32)
    ospec = pl.BlockSpec((1, tb, W), lambda b, i: (b, i, 0))
    return pl.pallas_call(
        _rwkv_pre_kernel,
        grid=(B, T // tb),
        in_specs=[
            pl.BlockSpec((1, tb, RWKV_PACKED), lambda b, i: (b, i, 0)),
            pl.BlockSpec((1, SUBLANES, RWKV_PACKED),
                         lambda b, i: (b, jnp.maximum(i * nprev - 1, 0), 0)),
            pl.BlockSpec((1, RWKV_PACKED), lambda b, i: (0, 0)),
            pl.BlockSpec((SUBLANES, W), lambda b, i: (0, 0)),
            pl.BlockSpec((LORA_PAD, W), lambda b, i: (0, 0)),
            pl.BlockSpec((LORA_PAD, W), lambda b, i: (0, 0)),
            pl.BlockSpec((G_LORA, W), lambda b, i: (0, 0)),
        ],
        out_specs=[ospec] * 7,
        out_shape=[out] * 7,
        compiler_params=_cparams(("parallel", "parallel")),
        name="rwkv_pre",
    )(p, p, mu_p, vec, w2_p, a2_p, g2.astype(BF16))


def _seg_sum(x, lo_half):
    s0 = jnp.sum(jnp.where(lo_half, x, 0.0), axis=1, keepdims=True)
    s1 = jnp.sum(jnp.where(lo_half, 0.0, x), axis=1, keepdims=True)
    return jnp.where(lo_half, s0, s1)


def _rwkv_scan_kernel(r_ref, w_ref, k_ref, v_ref, aa_ref, bb_ref, y_ref, s_ref, *, pairs, group):
    N = RWKV_HEAD
    tb = r_ref.shape[1]

    @pl.when(pl.program_id(2) == 0)
    def _():
        s_ref[...] = jnp.zeros_like(s_ref)

    lane = lax.broadcasted_iota(jnp.int32, (N, LANES), 1)
    sub = lax.broadcasted_iota(jnp.int32, (N, LANES), 0)
    lo_half = lane < N
    diag = (lane % N) == sub
    ones = _head_sum_matrix(N)

    def seg_sum_group(xs):
        parts = [_split2(x) for x in xs]
        stacked = jnp.concatenate([p[0] for p in parts] + [p[1] for p in parts], axis=0)
        out = _dot(stacked, ones)
        n = len(xs)
        return [out[q * N:(q + 1) * N] + out[(n + q) * N:(n + q + 1) * N] for q in range(n)]

    def seg_sum_all(xs):
        out = []
        for g in range(0, len(xs), group):
            out += seg_sum_group(xs[g:g + group])
        return out

    def body(t8, states):
        base = pl.multiple_of(t8 * SUBLANES, SUBLANES)
        S = list(states)
        sls = [(0, pl.ds(base, SUBLANES), slice(q * LANES, (q + 1) * LANES)) for q in range(pairs)]
        ops = [[ref[sl] for ref in (aa_ref, bb_ref, w_ref, k_ref, r_ref, v_ref)] for sl in sls]
        rows = [[] for _ in range(pairs)]
        for j in range(SUBLANES):
            row = slice(j, j + 1)
            sa = seg_sum_all([S[q] * ops[q][0][row] for q in range(pairs)])
            vcol = [_seg_sum(jnp.where(diag, ops[q][5][row], 0.0), lo_half) for q in range(pairs)]
            for q in range(pairs):
                a8, b8, w8, k8, r8, v8 = ops[q]
                S[q] = S[q] * w8[row] + sa[q] * b8[row] + vcol[q] * k8[row]
            y = seg_sum_all([S[q] * ops[q][4][row] for q in range(pairs)])
            for q in range(pairs):
                rows[q].append(jnp.sum(jnp.where(diag, y[q], 0.0), axis=0, keepdims=True))
        for q in range(pairs):
            y_ref[sls[q]] = jnp.concatenate(rows[q], axis=0)
        return tuple(S)

    init = tuple(s_ref[q] for q in range(pairs))
    final = lax.fori_loop(0, tb // SUBLANES, body, init)
    for q in range(pairs):
        s_ref[q] = final[q]


def rwkv_scan(r, w, k, v, aa, bb, *, tb=256, pairs=8, group=1):
    B, T, W = r.shape
    tb = min(tb, T)
    gw = pairs * LANES
    spec = pl.BlockSpec((1, tb, gw), lambda b, g, i: (b, i, g))
    return pl.pallas_call(
        functools.partial(_rwkv_scan_kernel, pairs=pairs, group=group),
        grid=(B, W // gw, T // tb),
        in_specs=[spec] * 6,
        out_specs=spec,
        out_shape=jax.ShapeDtypeStruct((B, T, W), F32),
        scratch_shapes=[pltpu.VMEM((pairs, RWKV_HEAD, LANES), F32)],
        compiler_params=_cparams(("parallel", "parallel", "arbitrary")),
        name="rwkv_scan",
    )(r, w, k, v, aa, bb)


def _rwkv_post_kernel(y_ref, r_ref, k_ref, v_ref, g_ref, vec_ref, o_ref):
    W = RWKV_WIDTH
    ones = _head_sum_matrix(RWKV_HEAD)
    ln_w = vec_ref[0:1, :]
    ln_b = vec_ref[1:2, :]
    r_k = vec_ref[2:3, :]

    def head_sum(x):
        return jnp.concatenate(
            [_dot_exact_rhs(x[:, j * LANES:(j + 1) * LANES], ones) for j in range(W // LANES)],
            axis=1)

    y = y_ref[...]
    mean = head_sum(y) * (1.0 / RWKV_HEAD)
    d = y - mean
    var = head_sum(d * d) * (1.0 / RWKV_HEAD)
    yn = d * lax.rsqrt(var + GN_EPS) * ln_w + ln_b
    bonus = head_sum(r_ref[...] * k_ref[...] * r_k) * v_ref[...]
    o_ref[...] = ((yn + bonus) * g_ref[...]).astype(o_ref.dtype)


def rwkv_post(y, r, k, v, g, ln_w, ln_b, r_k, *, tb=256):
    M, W = y.shape
    tb = min(tb, M)
    vec = jnp.zeros((SUBLANES, W), F32)
    vec = vec.at[0].set(ln_w).at[1].set(ln_b).at[2].set(r_k.reshape(W))
    spec = pl.BlockSpec((tb, W), lambda i: (i, 0))
    return pl.pallas_call(
        _rwkv_post_kernel,
        grid=(M // tb,),
        in_specs=[spec] * 5 + [pl.BlockSpec((SUBLANES, W), lambda i: (0, 0))],
        out_specs=spec,
        out_shape=jax.ShapeDtypeStruct((M, W), BF16),
        compiler_params=_cparams(("parallel",)),
        name="rwkv_post",
    )(y, r, k, v, g, vec)


def _merge_kernel(ym_ref, yr_ref, gm_ref, gr_ref, pm_ref, pr_ref, o_ref):
    m = _dot(ym_ref[...], pm_ref[...])
    r = _dot(yr_ref[...], pr_ref[...])
    o_ref[...] = (jax.nn.sigmoid(gm_ref[...]) * m
                  + jax.nn.sigmoid(gr_ref[...]) * r).astype(o_ref.dtype)


def merge(y_m, y_r, gates, proj_m, proj_r, *, tm=512, tn=512):
    M, K = y_m.shape
    D = proj_m.shape[1]
    tm = min(tm, M)
    return pl.pallas_call(
        _merge_kernel,
        grid=(M // tm, D // tn),
        in_specs=[
            pl.BlockSpec((tm, K), lambda i, j: (i, 0)),
            pl.BlockSpec((tm, K), lambda i, j: (i, 0)),
            pl.BlockSpec((tm, tn), lambda i, j: (i, j)),
            pl.BlockSpec((tm, tn), lambda i, j: (i, D // tn + j)),
            pl.BlockSpec((K, tn), lambda i, j: (0, j)),
            pl.BlockSpec((K, tn), lambda i, j: (0, j)),
        ],
        out_specs=pl.BlockSpec((tm, tn), lambda i, j: (i, j)),
        out_shape=jax.ShapeDtypeStruct((M, D), BF16),
        compiler_params=_cparams(("parallel", "arbitrary")),
        name="merge",
    )(y_m, y_r, gates, gates, proj_m, proj_r)


def _out_proj_kernel(mix_ref, x_ref, w_ref, g_ref, h_ref, xn_ref):
    h = x_ref[...] + _dot(mix_ref[...], w_ref[...])
    h_ref[...] = h
    ms = jnp.mean(h * h, axis=-1, keepdims=True)
    xn_ref[...] = h * lax.rsqrt(ms + RMS_EPS) * g_ref[...]


def out_proj(mixed, x2d, w_out_bf16, gain, *, tm=256):
    M, D = x2d.shape
    tm = min(tm, M)
    spec = pl.BlockSpec((tm, D), lambda i: (i, 0))
    return pl.pallas_call(
        _out_proj_kernel,
        grid=(M // tm,),
        in_specs=[spec, spec, pl.BlockSpec((D, D), lambda i: (0, 0)),
                  pl.BlockSpec((1, D), lambda i: (0, 0))],
        out_specs=[spec, spec],
        out_shape=[jax.ShapeDtypeStruct((M, D), F32)] * 2,
        compiler_params=_cparams(("parallel",)),
        name="out_proj",
    )(mixed, x2d, w_out_bf16, gain.reshape(1, D))


def _matmul_split_kernel(a_ref, whi_ref, wlo_ref, o_ref, hi_ref, lo_ref):
    @pl.when(pl.program_id(1) == 0)
    def _():
        hi, lo = _split2(a_ref[...])
        hi_ref[...] = hi
        lo_ref[...] = lo

    o_ref[...] = (_dot(hi_ref[...], whi_ref[...]) + _dot(lo_ref[...], whi_ref[...])
                  + _dot(hi_ref[...], wlo_ref[...]))


def matmul_split(a, w, *, tm=512, tn=512):
    M, K = a.shape
    N = w.shape[1]
    tm = min(tm, M)
    whi = w.astype(BF16)
    wlo = (w - whi.astype(F32)).astype(BF16)
    return pl.pallas_call(
        _matmul_split_kernel,
        grid=(M // tm, N // tn),
        in_specs=[pl.BlockSpec((tm, K), lambda i, j: (i, 0)),
                  pl.BlockSpec((K, tn), lambda i, j: (0, j)),
                  pl.BlockSpec((K, tn), lambda i, j: (0, j))],
        out_specs=pl.BlockSpec((tm, tn), lambda i, j: (i, j)),
        out_shape=jax.ShapeDtypeStruct((M, N), F32),
        scratch_shapes=[pltpu.VMEM((tm, K), BF16), pltpu.VMEM((tm, K), BF16)],
        compiler_params=_cparams(("parallel", "arbitrary")),
        name="peer_query",
    )(a, whi, wlo)


def _route_kernel(q_ref, khi_ref, klo_ref, idx_ref, gate_ref, s_ref, t_ref, c_ref, e_ref):
    tn = q_ref.shape[0]
    KK = PEER_TOPK
    half = D_KEY // 2
    NEG = -jnp.inf
    GROUP = 8
    lane_i = lax.broadcasted_iota(jnp.int32, (tn, N_KEYS), 1)
    lane = lane_i.astype(F32)
    lane2 = lax.broadcasted_iota(jnp.int32, (tn, KK * KK), 1).astype(F32)

    for hp in range(2 * PEER_HEADS):
        qhi, qlo = _split2(q_ref[:, hp * half:(hp + 1) * half])
        p = hp % 2
        s_ref[hp] = _dot(qhi, khi_ref[p]) + _dot(qlo, khi_ref[p]) + _dot(qhi, klo_ref[p])
    t_ref[...] = jnp.zeros_like(t_ref)

    def sub_topk(kk, carry):
        ms, ps = [], []
        for g in range(0, 2 * PEER_HEADS, GROUP):
            ss = [s_ref[hp] for hp in range(g, g + GROUP)]
            mg = [jnp.max(s, axis=1, keepdims=True) for s in ss]
            pg = [jnp.min(jnp.where(s == m, lane, float(N_KEYS)), axis=1, keepdims=True)
                  for s, m in zip(ss, mg)]
            for i, hp in enumerate(range(g, g + GROUP)):
                s_ref[hp] = jnp.where(lane == pg[i], NEG, ss[i])
            ms += mg
            ps += pg
        for p in range(2):
            vals = t_ref[2 * p]
            poss = t_ref[2 * p + 1]
            for h in range(PEER_HEADS):
                sel = lane_i == (h * KK + kk)
                vals = jnp.where(sel, ms[2 * h + p], vals)
                poss = jnp.where(sel, ps[2 * h + p], poss)
            t_ref[2 * p] = vals
            t_ref[2 * p + 1] = poss
        return carry

    lax.fori_loop(0, KK, sub_topk, 0)

    r = lax.broadcasted_iota(jnp.int32, (N_KEYS, KK * KK), 0)
    c = lax.broadcasted_iota(jnp.int32, (N_KEYS, KK * KK), 1)
    v1, i1, v2, i2 = t_ref[0], t_ref[1].astype(BF16), t_ref[2], t_ref[3].astype(BF16)
    for h in range(PEER_HEADS):
        sel1 = jnp.where(r == h * KK + c // KK, 1.0, 0.0).astype(BF16)
        sel2 = jnp.where(r == h * KK + c % KK, 1.0, 0.0).astype(BF16)
        c_ref[h] = _dot_exact_rhs(v1, sel1) + _dot_exact_rhs(v2, sel2)
        e_ref[h] = _dot(i1, sel1) * float(N_KEYS) + _dot(i2, sel2)

    def cand_topk(kk, carry):
        best, ids, top = carry
        cs = [c_ref[h] for h in range(PEER_HEADS)]
        ms = [jnp.max(cc, axis=1, keepdims=True) for cc in cs]
        ps = [jnp.min(jnp.where(cc == m, lane2, float(KK * KK)), axis=1, keepdims=True)
              for cc, m in zip(cs, ms)]
        hits = [lane2 == pp for pp in ps]
        eids = [jnp.max(jnp.where(hit, e_ref[h], -1.0), axis=1, keepdims=True)
                for h, hit in enumerate(hits)]
        for h in range(PEER_HEADS):
            c_ref[h] = jnp.where(hits[h], NEG, cs[h])
            sel = lane_i == (h * KK + kk)
            best = jnp.where(sel, ms[h], best)
            ids = jnp.where(sel, eids[h], ids)
        return best, ids, jnp.where(kk == 0, best, top)

    zero = jnp.zeros((tn, N_KEYS), F32)
    best, ids, top = lax.fori_loop(0, KK, cand_topk, (zero, zero, zero))
    head_max = _dot_exact_rhs(jnp.where(lane_i % KK == 0, top, 0.0), _head_sum_matrix(KK))
    e = jnp.exp(best - head_max)
    denom = _dot_exact_rhs(e, _head_sum_matrix(KK))
    gate_ref[...] = e / denom
    idx_ref[...] = ids.astype(jnp.int32)


def peer_route(q, sub_keys, *, tn=128):
    M = q.shape[0]
    tn = min(tn, M)
    kt = jnp.swapaxes(sub_keys, 1, 2)
    khi = kt.astype(BF16)
    klo = (kt - khi.astype(F32)).astype(BF16)
    kspec = pl.BlockSpec((2, D_KEY // 2, N_KEYS), lambda i: (0, 0, 0))
    ospec = pl.BlockSpec((tn, N_KEYS), lambda i: (i, 0))
    KK2 = PEER_TOPK * PEER_TOPK
    return pl.pallas_call(
        _route_kernel,
        grid=(M // tn,),
        in_specs=[pl.BlockSpec((tn, PEER_HEADS * D_KEY), lambda i: (i, 0)), kspec, kspec],
        out_specs=[ospec, ospec],
        out_shape=[jax.ShapeDtypeStruct((M, N_KEYS), jnp.int32),
                   jax.ShapeDtypeStruct((M, N_KEYS), F32)],
        scratch_shapes=[
            pltpu.VMEM((2 * PEER_HEADS, tn, N_KEYS), F32),
            pltpu.VMEM((4, tn, N_KEYS), F32),
            pltpu.VMEM((PEER_HEADS, tn, KK2), F32),
            pltpu.VMEM((PEER_HEADS, tn, KK2), F32),
        ],
        compiler_params=_cparams(("parallel",)),
        name="peer_route",
    )(q, khi, klo)


def _gelu_exact(x):
    return 0.5 * x * (1.0 + lax.erf(x * (2.0 ** -0.5)))


def _peer_expert_kernel(idx_ref, gate_ref, xn_ref, h_ref, gain_ref, uv_ref, o_ref,
                        buf_ref, acc_ref, sem):
    tb, D = xn_ref.shape
    E = idx_ref.shape[2]
    nw = D // (2 * LANES)
    step = pl.program_id(0)

    def slot_copy(slot):
        return pltpu.make_async_copy(buf_ref.at[slot], buf_ref.at[slot], sem.at[slot])

    def unpack(words):
        lo = lax.bitcast_convert_type(words << 16, F32)
        hi = lax.bitcast_convert_type(words & jnp.uint32(0xFFFF0000), F32)
        return lo, hi

    def issue(t, slot):
        for e in range(E):
            pltpu.make_async_copy(uv_ref.at[idx_ref[0, t, e]], buf_ref.at[slot, :, e, :],
                                  sem.at[slot]).start()

    @pl.when(step == 0)
    def _():
        for t in range(PEER_LOOKAHEAD):
            issue(t, t)

    eye = (lax.broadcasted_iota(jnp.int32, (E, E), 0)
           == lax.broadcasted_iota(jnp.int32, (E, E), 1))

    def body(t8, carry):
        base = pl.multiple_of(t8 * SUBLANES, SUBLANES)
        x8 = xn_ref[pl.ds(base, SUBLANES), :]
        g8 = gate_ref[pl.ds(base, SUBLANES), :]
        rows = []
        for j in range(SUBLANES):
            slot = j % PEER_SLOTS
            issue(base + j + PEER_LOOKAHEAD, (j + PEER_LOOKAHEAD) % PEER_SLOTS)
            slot_copy(slot).wait()
            part = jnp.zeros((E, LANES), F32)
            for s in range(nw):
                lo, hi = unpack(buf_ref[slot, s])
                part = (part + lo * x8[j:j + 1, s * LANES:(s + 1) * LANES]
                        + hi * x8[j:j + 1, (nw + s) * LANES:(nw + s + 1) * LANES])
            act = jnp.sum(part, axis=1, keepdims=True)
            g_col = jnp.sum(jnp.where(eye, g8[j:j + 1], 0.0), axis=1, keepdims=True)
            c = g_col * _gelu_exact(act)
            out_lo, out_hi = [], []
            for s in range(nw):
                lo, hi = unpack(buf_ref[slot, nw + s])
                out_lo.append(jnp.sum(lo * c, axis=0, keepdims=True))
                out_hi.append(jnp.sum(hi * c, axis=0, keepdims=True))
            rows.append(jnp.concatenate(out_lo + out_hi, axis=1))
        acc_ref[pl.ds(base, SUBLANES), :] = jnp.concatenate(rows, axis=0)
        return carry

    lax.fori_loop(0, tb // SUBLANES, body, 0)

    @pl.when(step == pl.num_programs(0) - 1)
    def _():
        for t in range(PEER_LOOKAHEAD):
            slot_copy(t % PEER_SLOTS).wait()

    hh = h_ref[...] + acc_ref[...]
    ms = jnp.mean(hh * hh, axis=-1, keepdims=True)
    o_ref[...] = hh * lax.rsqrt(ms + RMS_EPS) * gain_ref[...]


def pack_expert_rows(u, v):
    def pack(t):
        n, d = t.shape
        bits = lax.bitcast_convert_type(t.astype(BF16), jnp.uint16).astype(jnp.uint32)
        bits = bits.reshape(n, 2, d // (2 * LANES), LANES)
        return bits[:, 0] | (bits[:, 1] << 16)

    return jnp.concatenate([pack(u), pack(v)], axis=1)


def peer_experts(idx, gate, xn, h, gain, uv, *, tb=32):
    M, D = xn.shape
    E = idx.shape[1]
    tb = min(tb, M)
    assert tb % PEER_SLOTS == 0 and M % tb == 0
    nb = M // tb
    idx3 = idx.reshape(nb, tb, E)
    ahead = jnp.concatenate([idx3[1:, :PEER_SLOTS], idx3[-1:, :PEER_SLOTS]], axis=0)
    idx_ext = jnp.concatenate([idx3, ahead], axis=1)
    spec = pl.BlockSpec((tb, D), lambda i: (i, 0))
    return pl.pallas_call(
        _peer_expert_kernel,
        grid=(nb,),
        in_specs=[
            pl.BlockSpec((1, tb + PEER_SLOTS, E), lambda i: (i, 0, 0), memory_space=pltpu.SMEM),
            pl.BlockSpec((tb, E), lambda i: (i, 0)),
            spec, spec,
            pl.BlockSpec((1, D), lambda i: (0, 0)),
            pl.BlockSpec(memory_space=pl.ANY),
        ],
        out_specs=spec,
        out_shape=jax.ShapeDtypeStruct((M, D), F32),
        scratch_shapes=[
            pltpu.VMEM((PEER_SLOTS, D // LANES, E, LANES), jnp.uint32),
            pltpu.VMEM((tb, D), F32),
            pltpu.SemaphoreType.DMA((PEER_SLOTS,)),
        ],
        compiler_params=_cparams(("arbitrary",)),
        name="peer_experts",
    )(idx_ext, gate, xn, h, gain.reshape(1, D), uv)


def _pack_w_in(w_in):
    D = w_in.shape[0]
    W = RWKV_WIDTH
    o = 0
    qk = w_in[:, o:o + 2 * MLSTM_QK]; o += 2 * MLSTM_QK
    v = w_in[:, o:o + MLSTM_V]; o += MLSTM_V
    og = w_in[:, o:o + MLSTM_V]; o += MLSTM_V
    ifg = w_in[:, o:o + 2 * MLSTM_HEADS]; o += 2 * MLSTM_HEADS
    rkv = w_in[:, o:o + 3 * W]; o += 3 * W
    wl = w_in[:, o:o + W_LORA]; o += W_LORA
    al = w_in[:, o:o + A_LORA]; o += A_LORA
    gl = w_in[:, o:o + G_LORA]; o += G_LORA
    gates = w_in[:, o:]

    def pad(t, n):
        return jnp.pad(t, ((0, 0), (0, n - t.shape[1])))

    w_mlstm = jnp.concatenate([qk, v, og, pad(ifg, LANES)], axis=1)
    w_rwkv = jnp.concatenate([rkv, pad(wl, LORA_PAD), pad(al, LORA_PAD), gl], axis=1)
    assert w_mlstm.shape == (D, MLSTM_PACKED) and w_rwkv.shape == (D, RWKV_PACKED)
    return w_mlstm.astype(BF16), w_rwkv.astype(BF16), gates.astype(BF16)


def kernel(x, norm_mix_gain, w_in, mlstm_conv, mlstm_b_i, mlstm_b_f, rwkv_mu, rwkv_w0, rwkv_w2,
           rwkv_a0, rwkv_a2, rwkv_g2, rwkv_k_k, rwkv_k_a, rwkv_r_k, rwkv_ln_w, rwkv_ln_b,
           proj_mlstm, proj_rwkv, w_out, norm_ffn_gain, peer_w_query, peer_sub_keys,
           peer_u, peer_v, norm_final_gain):
    B, T, D = x.shape
    assert w_in.shape[0] == 1, "the output norm is fused into the single layer's PEER kernel"
    l = 0
    x2d = x.reshape(B * T, D)
    w_mlstm, w_rwkv, w_gates = _pack_w_in(w_in[l])
    p_m = norm_matmul(x2d, norm_mix_gain[l], w_mlstm, tn=MLSTM_PACKED // 3).reshape(B, T, MLSTM_PACKED)
    p_r = norm_matmul(x2d, norm_mix_gain[l], w_rwkv, tn=RWKV_PACKED // 4).reshape(B, T, RWKV_PACKED)
    p_g = norm_matmul(x2d, norm_mix_gain[l], w_gates, tn=1024)

    y_m = mlstm_branch(p_m, mlstm_conv[l], mlstm_b_i[l], mlstm_b_f[l])
    r, w, k, v, aa, bb, g = rwkv_pre(p_r, rwkv_mu[l], rwkv_w0[l], rwkv_w2[l], rwkv_a0[l],
                                     rwkv_a2[l], rwkv_g2[l], rwkv_k_k[l], rwkv_k_a[l])
    y = rwkv_scan(r, w, k, v, aa, bb)

    def flat(t):
        return t.reshape(B * T, RWKV_WIDTH)

    y_r = rwkv_post(flat(y), flat(r), flat(k), flat(v), flat(g),
                    rwkv_ln_w[l], rwkv_ln_b[l], rwkv_r_k[l])
    mixed = merge(y_m.reshape(B * T, MLSTM_V), y_r, p_g,
                  proj_mlstm[l].astype(BF16), proj_rwkv[l].astype(BF16))
    h2d, xn = out_proj(mixed, x2d, w_out[l].astype(BF16), norm_ffn_gain[l])
    q = matmul_split(xn, peer_w_query[l])
    idx, gate = peer_route(q, peer_sub_keys[l])
    out = peer_experts(idx, gate, xn, h2d, norm_final_gain, pack_expert_rows(peer_u[l], peer_v[l]))
    return out.reshape(B, T, D)
```

```python
import functools

import jax
import jax.numpy as jnp
from jax import lax
from jax.experimental import pallas as pl
from jax.experimental.pallas import tpu as pltpu

F32 = jnp.float32
BF16 = jnp.bfloat16

LANES = 128
SUBLANES = 8
VMEM_LIMIT_BYTES = 48 * 1024 * 1024

MLSTM_HEADS = 4
MLSTM_DK = 256
MLSTM_CONV = 4
MLSTM_CHUNK = 64
RWKV_HEADS = 16
RWKV_HEAD = 64
RWKV_WIDTH = RWKV_HEADS * RWKV_HEAD
W_LORA = 96
A_LORA = 96
G_LORA = 256
PEER_HEADS = 8
N_KEYS = 128
PEER_TOPK = 16
D_KEY = 256
RMS_EPS = 1e-6
GN_EPS = 64e-5
L2_EPS = 1e-12

PEER_SLOTS = 8
PEER_LOOKAHEAD = 4

MLSTM_QK = MLSTM_HEADS * MLSTM_DK
MLSTM_V = MLSTM_QK
LORA_PAD = 128

COL_QK = 0
COL_V = COL_QK + 2 * MLSTM_QK
COL_O = COL_V + MLSTM_V
COL_IF = COL_O + MLSTM_V
MLSTM_PACKED = COL_IF + LANES
RWKV_PACKED = 3 * RWKV_WIDTH + 2 * LORA_PAD + G_LORA


def _cparams(sem):
    return pltpu.CompilerParams(dimension_semantics=sem, vmem_limit_bytes=VMEM_LIMIT_BYTES)


def _split2(x):
    hi = x.astype(BF16)
    lo = (x - hi.astype(F32)).astype(BF16)
    return hi, lo


def _split3(x):
    hi = x.astype(BF16)
    r1 = x - hi.astype(F32)
    mid = r1.astype(BF16)
    lo = (r1 - mid.astype(F32)).astype(BF16)
    return hi, mid, lo


def _dot(a, b):
    return jnp.dot(a, b, preferred_element_type=F32)


def _dot_exact_rhs(x, ones_bf16):
    hi, mid, lo = _split3(x)
    return _dot(hi, ones_bf16) + _dot(mid, ones_bf16) + _dot(lo, ones_bf16)


def _norm_matmul_kernel(x_ref, g_ref, w_ref, o_ref, xn_ref):
    @pl.when(pl.program_id(1) == 0)
    def _():
        x = x_ref[...]
        ms = jnp.mean(x * x, axis=-1, keepdims=True)
        xn_ref[...] = (x * lax.rsqrt(ms + RMS_EPS) * g_ref[...]).astype(BF16)

    o_ref[...] = _dot(xn_ref[...], w_ref[...])


def norm_matmul(x, gain, w_bf16, *, tm=512, tn=384):
    M, K = x.shape
    N = w_bf16.shape[1]
    tm = min(tm, M)
    return pl.pallas_call(
        _norm_matmul_kernel,
        grid=(M // tm, N // tn),
        in_specs=[
            pl.BlockSpec((tm, K), lambda i, j: (i, 0)),
            pl.BlockSpec((1, K), lambda i, j: (0, 0)),
            pl.BlockSpec((K, tn), lambda i, j: (0, j)),
        ],
        out_specs=pl.BlockSpec((tm, tn), lambda i, j: (i, j)),
        out_shape=jax.ShapeDtypeStruct((M, N), F32),
        scratch_shapes=[pltpu.VMEM((tm, K), BF16)],
        compiler_params=_cparams(("parallel", "arbitrary")),
        name="norm_matmul",
    )(x, gain.reshape(1, K), w_bf16)


def _mlstm_kernel(qk_ref, v_ref, o_ref, if_ref, conv_ref, bias_ref, y_ref,
                  ext_ref, c_ref, n_ref, m_ref):
    L = MLSTM_CHUNK
    DK = MLSTM_DK
    step = pl.program_id(1)

    @pl.when(step == 0)
    def _():
        ext_ref[0:SUBLANES, :] = jnp.zeros((SUBLANES, 2 * MLSTM_QK), F32)
        c_ref[...] = jnp.zeros_like(c_ref)
        n_ref[...] = jnp.zeros_like(n_ref)
        m_ref[...] = jnp.zeros_like(m_ref)

    ext_ref[SUBLANES:SUBLANES + L, :] = qk_ref[0]
    acc = jnp.zeros((L, 2 * MLSTM_QK), F32)
    for j in range(MLSTM_CONV):
        off = SUBLANES - (MLSTM_CONV - 1) + j
        acc = acc + ext_ref[off:off + L, :] * conv_ref[j:j + 1, :]
    ext_ref[0:SUBLANES, :] = qk_ref[0, L - SUBLANES:L, :]
    qk = acc * jax.nn.sigmoid(acc)

    row = lax.broadcasted_iota(jnp.int32, (L, L), 0)
    col = lax.broadcasted_iota(jnp.int32, (L, L), 1)
    causal = col <= row
    eye = col == row

    def to_row(x_col):
        return jnp.sum(jnp.where(eye, x_col, 0.0), axis=0, keepdims=True)

    gates = if_ref[0]
    for h in range(MLSTM_HEADS):
        q = qk[:, h * DK:(h + 1) * DK] * (DK ** -0.5)
        k = qk[:, MLSTM_QK + h * DK:MLSTM_QK + (h + 1) * DK]
        v = v_ref[0, :, h * DK:(h + 1) * DK]
        ig_col = gates[:, h:h + 1] + bias_ref[0:1, h:h + 1]
        lf_col = jax.nn.log_sigmoid(
            gates[:, MLSTM_HEADS + h:MLSTM_HEADS + h + 1]
            + bias_ref[1:2, h:h + 1])
        lf_row = to_row(lf_col)
        ig_row = to_row(ig_col)
        b_col = jnp.sum(jnp.where(causal, lf_row, 0.0), axis=1, keepdims=True)
        b_row = to_row(b_col)
        b_last = b_col[L - 1:L, :]
        m_prev = m_ref[h:h + 1, 0:1]
        C = c_ref[h]
        n_row = n_ref[h:h + 1, :]

        dmat = jnp.where(causal, b_col - b_row + ig_row, -jnp.inf)
        inter = b_col + m_prev
        m_t = jnp.maximum(inter, jnp.max(dmat, axis=1, keepdims=True))
        qb = q.astype(BF16)
        kb = k.astype(BF16)
        vb = v.astype(BF16)
        s = lax.dot_general(qb, kb, (((1,), (1,)), ((), ())),
                            preferred_element_type=F32) * jnp.exp(dmat - m_t)
        w_inter = jnp.exp(inter - m_t)
        num = _dot(s.astype(BF16), vb) + w_inter * _dot(qb, C.astype(BF16))
        qn = jnp.sum(q * n_row, axis=1, keepdims=True)
        den = jnp.sum(s, axis=1, keepdims=True) + w_inter * qn
        hh = num / jnp.maximum(jnp.abs(den), jnp.exp(-m_t))
        o = o_ref[0, :, h * DK:(h + 1) * DK]
        y_ref[0, :, h * DK:(h + 1) * DK] = (jax.nn.sigmoid(o) * hh).astype(y_ref.dtype)

        g_end = b_last - b_col + ig_col
        m_new = jnp.maximum(b_last + m_prev, jnp.max(g_end, axis=0, keepdims=True))
        decay = jnp.exp(b_last + m_prev - m_new)
        ws = jnp.exp(g_end - m_new)
        kw = k * ws
        c_ref[h] = decay * C + lax.dot_general(
            kw.astype(BF16), vb, (((0,), (0,)), ((), ())), preferred_element_type=F32)
        n_ref[h:h + 1, :] = decay * n_row + jnp.sum(kw, axis=0, keepdims=True)
        m_ref[h:h + 1, :] = jnp.broadcast_to(m_new, (1, LANES))


def mlstm_branch(p, conv_w, b_i, b_f):
    B, T, _ = p.shape
    L = MLSTM_CHUNK
    bias = jnp.zeros((SUBLANES, LANES), F32)
    bias = bias.at[0, :MLSTM_HEADS].set(b_i).at[1, :MLSTM_HEADS].set(b_f)
    nqk = 2 * MLSTM_QK
    return pl.pallas_call(
        _mlstm_kernel,
        grid=(B, T // L),
        in_specs=[
            pl.BlockSpec((1, L, nqk), lambda b, c: (b, c, COL_QK // nqk)),
            pl.BlockSpec((1, L, MLSTM_V), lambda b, c: (b, c, COL_V // MLSTM_V)),
            pl.BlockSpec((1, L, MLSTM_V), lambda b, c: (b, c, COL_O // MLSTM_V)),
            pl.BlockSpec((1, L, LANES), lambda b, c: (b, c, COL_IF // LANES)),
            pl.BlockSpec((MLSTM_CONV, nqk), lambda b, c: (0, 0)),
            pl.BlockSpec((SUBLANES, LANES), lambda b, c: (0, 0)),
        ],
        out_specs=pl.BlockSpec((1, L, MLSTM_V), lambda b, c: (b, c, 0)),
        out_shape=jax.ShapeDtypeStruct((B, T, MLSTM_V), BF16),
        scratch_shapes=[
            pltpu.VMEM((SUBLANES + L, nqk), F32),
            pltpu.VMEM((MLSTM_HEADS, MLSTM_DK, MLSTM_DK), F32),
            pltpu.VMEM((SUBLANES, MLSTM_DK), F32),
            pltpu.VMEM((SUBLANES, LANES), F32),
        ],
        compiler_params=_cparams(("parallel", "arbitrary")),
        name="mlstm",
    )(p, p, p, p, conv_w, bias)


def _head_sum_matrix(group):
    r = lax.broadcasted_iota(jnp.int32, (LANES, LANES), 0) // group
    c = lax.broadcasted_iota(jnp.int32, (LANES, LANES), 1) // group
    return jnp.where(r == c, 1.0, 0.0).astype(BF16)


def _rwkv_pre_kernel(p_ref, prev_ref, mu_ref, vec_ref, w2_ref, a2_ref, g2_ref,
                     r_ref, w_ref, k_ref, v_ref, aa_ref, bb_ref, g_ref):
    W = RWKV_WIDTH
    tb = p_ref.shape[1]
    p = p_ref[0]
    last = prev_ref[0, SUBLANES - 1:SUBLANES, :]
    last = jnp.where(pl.program_id(1) == 0, 0.0, last)
    rid = lax.broadcasted_iota(jnp.int32, (tb, 1), 0)
    shifted = jnp.where(rid == 0, last, pltpu.roll(p, 1, 0))
    p = p + (shifted - p) * mu_ref[...]

    r = p[:, 0:W]
    k = p[:, W:2 * W]
    v = p[:, 2 * W:3 * W]
    wl = p[:, 3 * W:3 * W + LORA_PAD]
    al = p[:, 3 * W + LORA_PAD:3 * W + 2 * LORA_PAD]
    gl = p[:, 3 * W + 2 * LORA_PAD:]
    w0 = vec_ref[0:1, :]
    a0 = vec_ref[1:2, :]
    k_k = vec_ref[2:3, :]
    k_a = vec_ref[3:4, :]

    wx = -(w0 + _dot(jnp.tanh(wl).astype(BF16), w2_ref[...]))
    softplus = jnp.maximum(wx, 0.0) + jnp.log1p(jnp.exp(-jnp.abs(wx)))
    w = -softplus - 0.5
    decay = jnp.exp(-jnp.exp(w))
    a = jax.nn.sigmoid(a0 + _dot(al.astype(BF16), a2_ref[...]))
    g = _dot(jax.nn.sigmoid(gl).astype(BF16), g2_ref[...])

    kk = k * k_k
    ones = _head_sum_matrix(RWKV_HEAD)
    sq = kk * kk
    ss = jnp.concatenate(
        [_dot_exact_rhs(sq[:, j * LANES:(j + 1) * LANES], ones) for j in range(W // LANES)],
        axis=1)
    kk = kk / jnp.maximum(jnp.sqrt(ss), L2_EPS)

    r_ref[0] = r
    w_ref[0] = decay
    k_ref[0] = k * (1.0 + (a - 1.0) * k_a)
    v_ref[0] = v
    aa_ref[0] = -kk
    bb_ref[0] = kk * a
    g_ref[0] = g


def rwkv_pre(p, mu, w0, w2, a0, a2, g2, k_k, k_a, *, tb=256):
    B, T, _ = p.shape
    W = RWKV_WIDTH
    tb = min(tb, T)
    mu_p = jnp.zeros((1, RWKV_PACKED), F32)
    mu_p = mu_p.at[0, :3 * W].set(mu[:3 * W])
    mu_p = mu_p.at[0, 3 * W:3 * W + W_LORA].set(mu[3 * W:3 * W + W_LORA])
    mu_p = mu_p.at[0, 3 * W + LORA_PAD:3 * W + LORA_PAD + A_LORA].set(
        mu[3 * W + W_LORA:3 * W + W_LORA + A_LORA])
    mu_p = mu_p.at[0, 3 * W + 2 * LORA_PAD:].set(mu[3 * W + W_LORA + A_LORA:])
    vec = jnp.zeros((SUBLANES, W), F32)
    vec = vec.at[0].set(w0).at[1].set(a0).at[2].set(k_k).at[3].set(k_a)
    w2_p = jnp.zeros((LORA_PAD, W), F32).at[:W_LORA].set(w2).astype(BF16)
    a2_p = jnp.zeros((LORA_PAD, W), F32).at[:A_LORA].set(a2).astype(BF16)
    nprev = tb // SUBLANES
    out = jax.ShapeDtypeStruct((B, T, W), F32)
    ospec = pl.BlockSpec((1, tb, W), lambda b, i: (b, i, 0))
    return pl.pallas_call(
        _rwkv_pre_kernel,
        grid=(B, T // tb),
        in_specs=[
            pl.BlockSpec((1, tb, RWKV_PACKED), lambda b, i: (b, i, 0)),
            pl.BlockSpec((1, SUBLANES, RWKV_PACKED),
                         lambda b, i: (b, jnp.maximum(i * nprev - 1, 0), 0)),
            pl.BlockSpec((1, RWKV_PACKED), lambda b, i: (0, 0)),
            pl.BlockSpec((SUBLANES, W), lambda b, i: (0, 0)),
            pl.BlockSpec((LORA_PAD, W), lambda b, i: (0, 0)),
            pl.BlockSpec((LORA_PAD, W), lambda b, i: (0, 0)),
            pl.BlockSpec((G_LORA, W), lambda b, i: (0, 0)),
        ],
        out_specs=[ospec] * 7,
        out_shape=[out] * 7,
        compiler_params=_cparams(("parallel", "parallel")),
        name="rwkv_pre",
    )(p, p, mu_p, vec, w2_p, a2_p, g2.astype(BF16))


def _seg_sum(x, lo_half):
    s0 = jnp.sum(jnp.where(lo_half, x, 0.0), axis=1, keepdims=True)
    s1 = jnp.sum(jnp.where(lo_half, 0.0, x), axis=1, keepdims=True)
    return jnp.where(lo_half, s0, s1)


def _rwkv_scan_kernel(r_ref, w_ref, k_ref, v_ref, aa_ref, bb_ref, y_ref, s_ref, *, pairs, group):
    N = RWKV_HEAD
    tb = r_ref.shape[1]

    @pl.when(pl.program_id(2) == 0)
    def _():
        s_ref[...] = jnp.zeros_like(s_ref)

    lane = lax.broadcasted_iota(jnp.int32, (N, LANES), 1)
    sub = lax.broadcasted_iota(jnp.int32, (N, LANES), 0)
    lo_half = lane < N
    diag = (lane % N) == sub
    ones = _head_sum_matrix(N)

    def seg_sum_group(xs):
        parts = [_split2(x) for x in xs]
        stacked = jnp.concatenate([p[0] for p in parts] + [p[1] for p in parts], axis=0)
        out = _dot(stacked, ones)
        n = len(xs)
        return [out[q * N:(q + 1) * N] + out[(n + q) * N:(n + q + 1) * N] for q in range(n)]

    def seg_sum_all(xs):
        out = []
        for g in range(0, len(xs), group):
            out += seg_sum_group(xs[g:g + group])
        return out

    def body(t8, states):
        base = pl.multiple_of(t8 * SUBLANES, SUBLANES)
        S = list(states)
        sls = [(0, pl.ds(base, SUBLANES), slice(q * LANES, (q + 1) * LANES)) for q in range(pairs)]
        ops = [[ref[sl] for ref in (aa_ref, bb_ref, w_ref, k_ref, r_ref, v_ref)] for sl in sls]
        rows = [[] for _ in range(pairs)]
        for j in range(SUBLANES):
            row = slice(j, j + 1)
            sa = seg_sum_all([S[q] * ops[q][0][row] for q in range(pairs)])
            vcol = [_seg_sum(jnp.where(diag, ops[q][5][row], 0.0), lo_half) for q in range(pairs)]
            for q in range(pairs):
                a8, b8, w8, k8, r8, v8 = ops[q]
                S[q] = S[q] * w8[row] + sa[q] * b8[row] + vcol[q] * k8[row]
            y = seg_sum_all([S[q] * ops[q][4][row] for q in range(pairs)])
            for q in range(pairs):
                rows[q].append(jnp.sum(jnp.where(diag, y[q], 0.0), axis=0, keepdims=True))
        for q in range(pairs):
            y_ref[sls[q]] = jnp.concatenate(rows[q], axis=0)
        return tuple(S)

    init = tuple(s_ref[q] for q in range(pairs))
    final = lax.fori_loop(0, tb // SUBLANES, body, init)
    for q in range(pairs):
        s_ref[q] = final[q]


def rwkv_scan(r, w, k, v, aa, bb, *, tb=256, pairs=8, group=1):
    B, T, W = r.shape
    tb = min(tb, T)
    gw = pairs * LANES
    spec = pl.BlockSpec((1, tb, gw), lambda b, g, i: (b, i, g))
    return pl.pallas_call(
        functools.partial(_rwkv_scan_kernel, pairs=pairs, group=group),
        grid=(B, W // gw, T // tb),
        in_specs=[spec] * 6,
        out_specs=spec,
        out_shape=jax.ShapeDtypeStruct((B, T, W), F32),
        scratch_shapes=[pltpu.VMEM((pairs, RWKV_HEAD, LANES), F32)],
        compiler_params=_cparams(("parallel", "parallel", "arbitrary")),
        name="rwkv_scan",
    )(r, w, k, v, aa, bb)


def _rwkv_post_kernel(y_ref, r_ref, k_ref, v_ref, g_ref, vec_ref, o_ref):
    W = RWKV_WIDTH
    ones = _head_sum_matrix(RWKV_HEAD)
    ln_w = vec_ref[0:1, :]
    ln_b = vec_ref[1:2, :]
    r_k = vec_ref[2:3, :]

    def head_sum(x):
        return jnp.concatenate(
            [_dot_exact_rhs(x[:, j * LANES:(j + 1) * LANES], ones) for j in range(W // LANES)],
            axis=1)

    y = y_ref[...]
    mean = head_sum(y) * (1.0 / RWKV_HEAD)
    d = y - mean
    var = head_sum(d * d) * (1.0 / RWKV_HEAD)
    yn = d * lax.rsqrt(var + GN_EPS) * ln_w + ln_b
    bonus = head_sum(r_ref[...] * k_ref[...] * r_k) * v_ref[...]
    o_ref[...] = ((yn + bonus) * g_ref[...]).astype(o_ref.dtype)


def rwkv_post(y, r, k, v, g, ln_w, ln_b, r_k, *, tb=256):
    M, W = y.shape
    tb = min(tb, M)
    vec = jnp.zeros((SUBLANES, W), F32)
    vec = vec.at[0].set(ln_w).at[1].set(ln_b).at[2].set(r_k.reshape(W))
    spec = pl.BlockSpec((tb, W), lambda i: (i, 0))
    return pl.pallas_call(
        _rwkv_post_kernel,
        grid=(M // tb,),
        in_specs=[spec] * 5 + [pl.BlockSpec((SUBLANES, W), lambda i: (0, 0))],
        out_specs=spec,
        out_shape=jax.ShapeDtypeStruct((M, W), BF16),
        compiler_params=_cparams(("parallel",)),
        name="rwkv_post",
    )(y, r, k, v, g, vec)


def _merge_kernel(ym_ref, yr_ref, gm_ref, gr_ref, pm_ref, pr_ref, o_ref):
    m = _dot(ym_ref[...], pm_ref[...])
    r = _dot(yr_ref[...], pr_ref[...])
    o_ref[...] = (jax.nn.sigmoid(gm_ref[...]) * m
                  + jax.nn.sigmoid(gr_ref[...]) * r).astype(o_ref.dtype)


def merge(y_m, y_r, gates, proj_m, proj_r, *, tm=512, tn=512):
    M, K = y_m.shape
    D = proj_m.shape[1]
    tm = min(tm, M)
    return pl.pallas_call(
        _merge_kernel,
        grid=(M // tm, D // tn),
        in_specs=[
            pl.BlockSpec((tm, K), lambda i, j: (i, 0)),
            pl.BlockSpec((tm, K), lambda i, j: (i, 0)),
            pl.BlockSpec((tm, tn), lambda i, j: (i, j)),
            pl.BlockSpec((tm, tn), lambda i, j: (i, D // tn + j)),
            pl.BlockSpec((K, tn), lambda i, j: (0, j)),
            pl.BlockSpec((K, tn), lambda i, j: (0, j)),
        ],
        out_specs=pl.BlockSpec((tm, tn), lambda i, j: (i, j)),
        out_shape=jax.ShapeDtypeStruct((M, D), BF16),
        compiler_params=_cparams(("parallel", "arbitrary")),
        name="merge",
    )(y_m, y_r, gates, gates, proj_m, proj_r)


def _out_proj_kernel(mix_ref, x_ref, w_ref, g_ref, h_ref, xn_ref):
    h = x_ref[...] + _dot(mix_ref[...], w_ref[...])
    h_ref[...] = h
    ms = jnp.mean(h * h, axis=-1, keepdims=True)
    xn_ref[...] = h * lax.rsqrt(ms + RMS_EPS) * g_ref[...]


def out_proj(mixed, x2d, w_out_bf16, gain, *, tm=256):
    M, D = x2d.shape
    tm = min(tm, M)
    spec = pl.BlockSpec((tm, D), lambda i: (i, 0))
    return pl.pallas_call(
        _out_proj_kernel,
        grid=(M // tm,),
        in_specs=[spec, spec, pl.BlockSpec((D, D), lambda i: (0, 0)),
                  pl.BlockSpec((1, D), lambda i: (0, 0))],
        out_specs=[spec, spec],
        out_shape=[jax.ShapeDtypeStruct((M, D), F32)] * 2,
        compiler_params=_cparams(("parallel",)),
        name="out_proj",
    )(mixed, x2d, w_out_bf16, gain.reshape(1, D))


def _matmul_split_kernel(a_ref, whi_ref, wlo_ref, o_ref, hi_ref, lo_ref):
    @pl.when(pl.program_id(1) == 0)
    def _():
        hi, lo = _split2(a_ref[...])
        hi_ref[...] = hi
        lo_ref[...] = lo

    o_ref[...] = (_dot(hi_ref[...], whi_ref[...]) + _dot(lo_ref[...], whi_ref[...])
                  + _dot(hi_ref[...], wlo_ref[...]))


def matmul_split(a, w, *, tm=512, tn=512):
    M, K = a.shape
    N = w.shape[1]
    tm = min(tm, M)
    whi = w.astype(BF16)
    wlo = (w - whi.astype(F32)).astype(BF16)
    return pl.pallas_call(
        _matmul_split_kernel,
        grid=(M // tm, N // tn),
        in_specs=[pl.BlockSpec((tm, K), lambda i, j: (i, 0)),
                  pl.BlockSpec((K, tn), lambda i, j: (0, j)),
                  pl.BlockSpec((K, tn), lambda i, j: (0, j))],
        out_specs=pl.BlockSpec((tm, tn), lambda i, j: (i, j)),
        out_shape=jax.ShapeDtypeStruct((M, N), F32),
        scratch_shapes=[pltpu.VMEM((tm, K), BF16), pltpu.VMEM((tm, K), BF16)],
        compiler_params=_cparams(("parallel", "arbitrary")),
        name="peer_query",
    )(a, whi, wlo)


def _route_kernel(q_ref, khi_ref, klo_ref, idx_ref, gate_ref,
                  s_ref, tv_ref, tp_ref, c_ref, e_ref, bv_ref, bi_ref):
    tn = q_ref.shape[0]
    KK = PEER_TOPK
    half = D_KEY // 2
    NEG = -jnp.inf
    key_id = lax.broadcasted_iota(jnp.int32, (N_KEYS, tn), 0).astype(F32)
    cand_id = lax.broadcasted_iota(jnp.int32, (KK * KK, tn), 0).astype(F32)
    rank = lax.broadcasted_iota(jnp.int32, (KK, tn), 0)

    def nt(a, b):
        return lax.dot_general(a, b, (((1,), (1,)), ((), ())), preferred_element_type=F32)

    for hp in range(2 * PEER_HEADS):
        qhi, qlo = _split2(q_ref[:, hp * half:(hp + 1) * half])
        p = hp % 2
        s_ref[hp] = nt(khi_ref[p], qhi) + nt(khi_ref[p], qlo) + nt(klo_ref[p], qhi)
    tv_ref[...] = jnp.zeros_like(tv_ref)
    tp_ref[...] = jnp.zeros_like(tp_ref)
    bv_ref[...] = jnp.zeros_like(bv_ref)
    bi_ref[...] = jnp.zeros_like(bi_ref)

    def sub_topk(kk, carry):
        for hp in range(2 * PEER_HEADS):
            s = s_ref[hp]
            m = jnp.max(s, axis=0, keepdims=True)
            pos = jnp.min(jnp.where(s == m, key_id, float(N_KEYS)), axis=0, keepdims=True)
            s_ref[hp] = jnp.where(key_id == pos, NEG, s)
            tv_ref[hp] = jnp.where(rank == kk, m, tv_ref[hp])
            tp_ref[hp] = jnp.where(rank == kk, pos, tp_ref[hp])
        return carry

    lax.fori_loop(0, KK, sub_topk, 0)

    for h in range(PEER_HEADS):
        v1, v2 = tv_ref[2 * h], tv_ref[2 * h + 1]
        p1, p2 = tp_ref[2 * h], tp_ref[2 * h + 1]
        for i in range(KK):
            c_ref[h, i * KK:(i + 1) * KK, :] = v1[i:i + 1, :] + v2
            e_ref[h, i * KK:(i + 1) * KK, :] = p1[i:i + 1, :] * float(N_KEYS) + p2

    def cand_topk(kk, carry):
        for h in range(PEER_HEADS):
            c = c_ref[h]
            m = jnp.max(c, axis=0, keepdims=True)
            pos = jnp.min(jnp.where(c == m, cand_id, float(KK * KK)), axis=0, keepdims=True)
            hit = cand_id == pos
            eid = jnp.max(jnp.where(hit, e_ref[h], -1.0), axis=0, keepdims=True)
            c_ref[h] = jnp.where(hit, NEG, c)
            rows = slice(h * KK, (h + 1) * KK)
            bv_ref[rows, :] = jnp.where(rank == kk, m, bv_ref[rows, :])
            bi_ref[rows, :] = jnp.where(rank == kk, eid, bi_ref[rows, :])
        return carry

    lax.fori_loop(0, KK, cand_topk, 0)

    for h in range(PEER_HEADS):
        rows = slice(h * KK, (h + 1) * KK)
        b = bv_ref[rows, :]
        e = jnp.exp(b - b[0:1, :])
        bv_ref[rows, :] = e / jnp.sum(e, axis=0, keepdims=True)
    gate_ref[...] = bv_ref[...].T
    idx_ref[...] = bi_ref[...].T.astype(jnp.int32)


def peer_route(q, sub_keys, *, tn=128):
    M = q.shape[0]
    tn = min(tn, M)
    khi = sub_keys.astype(BF16)
    klo = (sub_keys - khi.astype(F32)).astype(BF16)
    kspec = pl.BlockSpec((2, N_KEYS, D_KEY // 2), lambda i: (0, 0, 0))
    ospec = pl.BlockSpec((tn, N_KEYS), lambda i: (i, 0))
    KK2 = PEER_TOPK * PEER_TOPK
    return pl.pallas_call(
        _route_kernel,
        grid=(M // tn,),
        in_specs=[pl.BlockSpec((tn, PEER_HEADS * D_KEY), lambda i: (i, 0)), kspec, kspec],
        out_specs=[ospec, ospec],
        out_shape=[jax.ShapeDtypeStruct((M, N_KEYS), jnp.int32),
                   jax.ShapeDtypeStruct((M, N_KEYS), F32)],
        scratch_shapes=[
            pltpu.VMEM((2 * PEER_HEADS, N_KEYS, tn), F32),
            pltpu.VMEM((2 * PEER_HEADS, PEER_TOPK, tn), F32),
            pltpu.VMEM((2 * PEER_HEADS, PEER_TOPK, tn), F32),
            pltpu.VMEM((PEER_HEADS, KK2, tn), F32),
            pltpu.VMEM((PEER_HEADS, KK2, tn), F32),
            pltpu.VMEM((PEER_HEADS * PEER_TOPK, tn), F32),
            pltpu.VMEM((PEER_HEADS * PEER_TOPK, tn), F32),
        ],
        compiler_params=_cparams(("parallel",)),
        name="peer_route",
    )(q, khi, klo)


def _gelu_exact(x):
    return 0.5 * x * (1.0 + lax.erf(x * (2.0 ** -0.5)))


def _peer_expert_kernel(idx_ref, gate_ref, xn_ref, h_ref, gain_ref, uv_ref, o_ref,
                        buf_ref, acc_ref, sem):
    tb, D = xn_ref.shape
    E = idx_ref.shape[2]
    nw = D // (2 * LANES)
    step = pl.program_id(0)

    def slot_copy(slot):
        return pltpu.make_async_copy(buf_ref.at[slot], buf_ref.at[slot], sem.at[slot])

    def unpack(words):
        lo = lax.bitcast_convert_type(words << 16, F32)
        hi = lax.bitcast_convert_type(words & jnp.uint32(0xFFFF0000), F32)
        return lo, hi

    def issue(t, slot):
        for e in range(E):
            pltpu.make_async_copy(uv_ref.at[idx_ref[0, t, e]], buf_ref.at[slot, :, e, :],
                                  sem.at[slot]).start(priority=e % 2)

    @pl.when(step == 0)
    def _():
        for t in range(PEER_LOOKAHEAD):
            issue(t, t)

    eye = (lax.broadcasted_iota(jnp.int32, (E, E), 0)
           == lax.broadcasted_iota(jnp.int32, (E, E), 1))

    def body(t8, carry):
        base = pl.multiple_of(t8 * SUBLANES, SUBLANES)
        x8 = xn_ref[pl.ds(base, SUBLANES), :]
        g8 = gate_ref[pl.ds(base, SUBLANES), :]
        rows = []
        for j in range(SUBLANES):
            slot = j % PEER_SLOTS
            issue(base + j + PEER_LOOKAHEAD, (j + PEER_LOOKAHEAD) % PEER_SLOTS)
            slot_copy(slot).wait()
            part = jnp.zeros((E, LANES), F32)
            for s in range(nw):
                lo, hi = unpack(buf_ref[slot, s])
                part = (part + lo * x8[j:j + 1, s * LANES:(s + 1) * LANES]
                        + hi * x8[j:j + 1, (nw + s) * LANES:(nw + s + 1) * LANES])
            act = jnp.sum(part, axis=1, keepdims=True)
            g_col = jnp.sum(jnp.where(eye, g8[j:j + 1], 0.0), axis=1, keepdims=True)
            c = g_col * _gelu_exact(act)
            out_lo, out_hi = [], []
            for s in range(nw):
                lo, hi = unpack(buf_ref[slot, nw + s])
                out_lo.append(jnp.sum(lo * c, axis=0, keepdims=True))
                out_hi.append(jnp.sum(hi * c, axis=0, keepdims=True))
            rows.append(jnp.concatenate(out_lo + out_hi, axis=1))
        acc_ref[pl.ds(base, SUBLANES), :] = jnp.concatenate(rows, axis=0)
        return carry

    lax.fori_loop(0, tb // SUBLANES, body, 0)

    @pl.when(step == pl.num_programs(0) - 1)
    def _():
        for t in range(PEER_LOOKAHEAD):
            slot_copy(t % PEER_SLOTS).wait()

    hh = h_ref[...] + acc_ref[...]
    ms = jnp.mean(hh * hh, axis=-1, keepdims=True)
    o_ref[...] = hh * lax.rsqrt(ms + RMS_EPS) * gain_ref[...]


def pack_expert_rows(u, v):
    def pack(t):
        n, d = t.shape
        bits = lax.bitcast_convert_type(t.astype(BF16), jnp.uint16).astype(jnp.uint32)
        bits = bits.reshape(n, 2, d // (2 * LANES), LANES)
        return bits[:, 0] | (bits[:, 1] << 16)

    return jnp.concatenate([pack(u), pack(v)], axis=1)


def peer_experts(idx, gate, xn, h, gain, uv, *, tb=32):
    M, D = xn.shape
    E = idx.shape[1]
    tb = min(tb, M)
    assert tb % PEER_SLOTS == 0 and M % tb == 0
    nb = M // tb
    idx3 = idx.reshape(nb, tb, E)
    ahead = jnp.concatenate([idx3[1:, :PEER_SLOTS], idx3[-1:, :PEER_SLOTS]], axis=0)
    idx_ext = jnp.concatenate([idx3, ahead], axis=1)
    spec = pl.BlockSpec((tb, D), lambda i: (i, 0))
    return pl.pallas_call(
        _peer_expert_kernel,
        grid=(nb,),
        in_specs=[
            pl.BlockSpec((1, tb + PEER_SLOTS, E), lambda i: (i, 0, 0), memory_space=pltpu.SMEM),
            pl.BlockSpec((tb, E), lambda i: (i, 0)),
            spec, spec,
            pl.BlockSpec((1, D), lambda i: (0, 0)),
            pl.BlockSpec(memory_space=pl.ANY),
        ],
        out_specs=spec,
        out_shape=jax.ShapeDtypeStruct((M, D), F32),
        scratch_shapes=[
            pltpu.VMEM((PEER_SLOTS, D // LANES, E, LANES), jnp.uint32),
            pltpu.VMEM((tb, D), F32),
            pltpu.SemaphoreType.DMA((PEER_SLOTS,)),
        ],
        compiler_params=_cparams(("arbitrary",)),
        name="peer_experts",
    )(idx_ext, gate, xn, h, gain.reshape(1, D), uv)


def _pack_w_in(w_in):
    D = w_in.shape[0]
    W = RWKV_WIDTH
    o = 0
    qk = w_in[:, o:o + 2 * MLSTM_QK]; o += 2 * MLSTM_QK
    v = w_in[:, o:o + MLSTM_V]; o += MLSTM_V
    og = w_in[:, o:o + MLSTM_V]; o += MLSTM_V
    ifg = w_in[:, o:o + 2 * MLSTM_HEADS]; o += 2 * MLSTM_HEADS
    rkv = w_in[:, o:o + 3 * W]; o += 3 * W
    wl = w_in[:, o:o + W_LORA]; o += W_LORA
    al = w_in[:, o:o + A_LORA]; o += A_LORA
    gl = w_in[:, o:o + G_LORA]; o += G_LORA
    gates = w_in[:, o:]

    def pad(t, n):
        return jnp.pad(t, ((0, 0), (0, n - t.shape[1])))

    w_mlstm = jnp.concatenate([qk, v, og, pad(ifg, LANES)], axis=1)
    w_rwkv = jnp.concatenate([rkv, pad(wl, LORA_PAD), pad(al, LORA_PAD), gl], axis=1)
    assert w_mlstm.shape == (D, MLSTM_PACKED) and w_rwkv.shape == (D, RWKV_PACKED)
    return w_mlstm.astype(BF16), w_rwkv.astype(BF16), gates.astype(BF16)


def kernel(x, norm_mix_gain, w_in, mlstm_conv, mlstm_b_i, mlstm_b_f, rwkv_mu, rwkv_w0, rwkv_w2,
           rwkv_a0, rwkv_a2, rwkv_g2, rwkv_k_k, rwkv_k_a, rwkv_r_k, rwkv_ln_w, rwkv_ln_b,
           proj_mlstm, proj_rwkv, w_out, norm_ffn_gain, peer_w_query, peer_sub_keys,
           peer_u, peer_v, norm_final_gain):
    B, T, D = x.shape
    assert w_in.shape[0] == 1, "the output norm is fused into the single layer's PEER kernel"
    l = 0
    x2d = x.reshape(B * T, D)
    w_mlstm, w_rwkv, w_gates = _pack_w_in(w_in[l])
    p_m = norm_matmul(x2d, norm_mix_gain[l], w_mlstm, tn=MLSTM_PACKED // 3).reshape(B, T, MLSTM_PACKED)
    p_r = norm_matmul(x2d, norm_mix_gain[l], w_rwkv, tn=RWKV_PACKED // 4).reshape(B, T, RWKV_PACKED)
    p_g = norm_matmul(x2d, norm_mix_gain[l], w_gates, tn=1024)

    y_m = mlstm_branch(p_m, mlstm_conv[l], mlstm_b_i[l], mlstm_b_f[l])
    r, w, k, v, aa, bb, g = rwkv_pre(p_r, rwkv_mu[l], rwkv_w0[l], rwkv_w2[l], rwkv_a0[l],
                                     rwkv_a2[l], rwkv_g2[l], rwkv_k_k[l], rwkv_k_a[l])
    y = rwkv_scan(r, w, k, v, aa, bb)

    def flat(t):
        return t.reshape(B * T, RWKV_WIDTH)

    y_r = rwkv_post(flat(y), flat(r), flat(k), flat(v), flat(g),
                    rwkv_ln_w[l], rwkv_ln_b[l], rwkv_r_k[l])
    mixed = merge(y_m.reshape(B * T, MLSTM_V), y_r, p_g,
                  proj_mlstm[l].astype(BF16), proj_rwkv[l].astype(BF16))
    h2d, xn = out_proj(mixed, x2d, w_out[l].astype(BF16), norm_ffn_gain[l])
    q = matmul_split(xn, peer_w_query[l])
    idx, gate = peer_route(q, peer_sub_keys[l])
    out = peer_experts(idx, gate, xn, h2d, norm_final_gain, pack_expert_rows(peer_u[l], peer_v[l]))
    return out.reshape(B, T, D)
```

```python
import jax
import jax.numpy as jnp
from jax import lax
from jax.experimental import pallas as pl
from jax.experimental.pallas import tpu as pltpu

F32 = jnp.float32
BF16 = jnp.bfloat16

LANES = 128
SUBLANES = 8
VMEM_LIMIT_BYTES = 48 * 1024 * 1024

MLSTM_HEADS = 4
MLSTM_DK = 256
MLSTM_CONV = 4
MLSTM_CHUNK = 64
RWKV_HEADS = 16
RWKV_HEAD = 64
RWKV_WIDTH = RWKV_HEADS * RWKV_HEAD
W_LORA = 96
A_LORA = 96
G_LORA = 256
PEER_HEADS = 8
N_KEYS = 128
PEER_TOPK = 16
D_KEY = 256
RMS_EPS = 1e-6
GN_EPS = 64e-5
L2_EPS = 1e-12

PEER_SLOTS = 8
PEER_LOOKAHEAD = 6

MLSTM_QK = MLSTM_HEADS * MLSTM_DK
MLSTM_V = MLSTM_QK
LORA_PAD = 128

COL_QK = 0
COL_V = COL_QK + 2 * MLSTM_QK
COL_O = COL_V + MLSTM_V
COL_IF = COL_O + MLSTM_V
MLSTM_PACKED = COL_IF + LANES
RWKV_PACKED = 3 * RWKV_WIDTH + 2 * LORA_PAD + G_LORA


def _cparams(sem):
    return pltpu.CompilerParams(dimension_semantics=sem, vmem_limit_bytes=VMEM_LIMIT_BYTES)


def _split2(x):
    hi = x.astype(BF16)
    lo = (x - hi.astype(F32)).astype(BF16)
    return hi, lo


def _split3(x):
    hi = x.astype(BF16)
    r1 = x - hi.astype(F32)
    mid = r1.astype(BF16)
    lo = (r1 - mid.astype(F32)).astype(BF16)
    return hi, mid, lo


def _dot(a, b):
    return jnp.dot(a, b, preferred_element_type=F32)


def _dot_exact_rhs(x, ones_bf16):
    hi, mid, lo = _split3(x)
    return _dot(hi, ones_bf16) + _dot(mid, ones_bf16) + _dot(lo, ones_bf16)


def _norm_matmul_kernel(x_ref, g_ref, w_ref, o_ref, xn_ref):
    @pl.when(pl.program_id(1) == 0)
    def _():
        x = x_ref[...]
        ms = jnp.mean(x * x, axis=-1, keepdims=True)
        xn_ref[...] = (x * lax.rsqrt(ms + RMS_EPS) * g_ref[...]).astype(BF16)

    o_ref[...] = _dot(xn_ref[...], w_ref[...])


def norm_matmul(x, gain, w_bf16, *, tm=512, tn=384):
    M, K = x.shape
    N = w_bf16.shape[1]
    tm = min(tm, M)
    return pl.pallas_call(
        _norm_matmul_kernel,
        grid=(M // tm, N // tn),
        in_specs=[
            pl.BlockSpec((tm, K), lambda i, j: (i, 0)),
            pl.BlockSpec((1, K), lambda i, j: (0, 0)),
            pl.BlockSpec((K, tn), lambda i, j: (0, j)),
        ],
        out_specs=pl.BlockSpec((tm, tn), lambda i, j: (i, j)),
        out_shape=jax.ShapeDtypeStruct((M, N), F32),
        scratch_shapes=[pltpu.VMEM((tm, K), BF16)],
        compiler_params=_cparams(("parallel", "arbitrary")),
        name="norm_matmul",
    )(x, gain.reshape(1, K), w_bf16)


def _mlstm_kernel(qk_ref, v_ref, o_ref, if_ref, conv_ref, bias_ref, y_ref,
                  ext_ref, c_ref, n_ref, m_ref):
    L = MLSTM_CHUNK
    DK = MLSTM_DK
    step = pl.program_id(1)

    @pl.when(step == 0)
    def _():
        ext_ref[0:SUBLANES, :] = jnp.zeros((SUBLANES, 2 * MLSTM_QK), F32)
        c_ref[...] = jnp.zeros_like(c_ref)
        n_ref[...] = jnp.zeros_like(n_ref)
        m_ref[...] = jnp.zeros_like(m_ref)

    ext_ref[SUBLANES:SUBLANES + L, :] = qk_ref[0]
    acc = jnp.zeros((L, 2 * MLSTM_QK), F32)
    for j in range(MLSTM_CONV):
        off = SUBLANES - (MLSTM_CONV - 1) + j
        acc = acc + ext_ref[off:off + L, :] * conv_ref[j:j + 1, :]
    ext_ref[0:SUBLANES, :] = qk_ref[0, L - SUBLANES:L, :]
    qk = acc * jax.nn.sigmoid(acc)

    row = lax.broadcasted_iota(jnp.int32, (L, L), 0)
    col = lax.broadcasted_iota(jnp.int32, (L, L), 1)
    causal = col <= row
    eye = col == row

    def to_row(x_col):
        return jnp.sum(jnp.where(eye, x_col, 0.0), axis=0, keepdims=True)

    gates = if_ref[0]
    for h in range(MLSTM_HEADS):
        q = qk[:, h * DK:(h + 1) * DK] * (DK ** -0.5)
        k = qk[:, MLSTM_QK + h * DK:MLSTM_QK + (h + 1) * DK]
        v = v_ref[0, :, h * DK:(h + 1) * DK]
        ig_col = gates[:, h:h + 1] + bias_ref[0:1, h:h + 1]
        lf_col = jax.nn.log_sigmoid(
            gates[:, MLSTM_HEADS + h:MLSTM_HEADS + h + 1]
            + bias_ref[1:2, h:h + 1])
        lf_row = to_row(lf_col)
        ig_row = to_row(ig_col)
        b_col = jnp.sum(jnp.where(causal, lf_row, 0.0), axis=1, keepdims=True)
        b_row = to_row(b_col)
        b_last = b_col[L - 1:L, :]
        m_prev = m_ref[h:h + 1, 0:1]
        C = c_ref[h]
        n_row = n_ref[h:h + 1, :]

        dmat = jnp.where(causal, b_col - b_row + ig_row, -jnp.inf)
        inter = b_col + m_prev
        m_t = jnp.maximum(inter, jnp.max(dmat, axis=1, keepdims=True))
        qb = q.astype(BF16)
        kb = k.astype(BF16)
        vb = v.astype(BF16)
        s = lax.dot_general(qb, kb, (((1,), (1,)), ((), ())),
                            preferred_element_type=F32) * jnp.exp(dmat - m_t)
        w_inter = jnp.exp(inter - m_t)
        num = _dot(s.astype(BF16), vb) + w_inter * _dot(qb, C.astype(BF16))
        qn = jnp.sum(q * n_row, axis=1, keepdims=True)
        den = jnp.sum(s, axis=1, keepdims=True) + w_inter * qn
        hh = num / jnp.maximum(jnp.abs(den), jnp.exp(-m_t))
        o = o_ref[0, :, h * DK:(h + 1) * DK]
        y_ref[0, :, h * DK:(h + 1) * DK] = (jax.nn.sigmoid(o) * hh).astype(y_ref.dtype)

        g_end = b_last - b_col + ig_col
        m_new = jnp.maximum(b_last + m_prev, jnp.max(g_end, axis=0, keepdims=True))
        decay = jnp.exp(b_last + m_prev - m_new)
        ws = jnp.exp(g_end - m_new)
        kw = k * ws
        c_ref[h] = decay * C + lax.dot_general(
            kw.astype(BF16), vb, (((0,), (0,)), ((), ())), preferred_element_type=F32)
        n_ref[h:h + 1, :] = decay * n_row + jnp.sum(kw, axis=0, keepdims=True)
        m_ref[h:h + 1, :] = jnp.broadcast_to(m_new, (1, LANES))


def mlstm_branch(p, conv_w, b_i, b_f):
    B, T, _ = p.shape
    L = MLSTM_CHUNK
    bias = jnp.zeros((SUBLANES, LANES), F32)
    bias = bias.at[0, :MLSTM_HEADS].set(b_i).at[1, :MLSTM_HEADS].set(b_f)
    nqk = 2 * MLSTM_QK
    return pl.pallas_call(
        _mlstm_kernel,
        grid=(B, T // L),
        in_specs=[
            pl.BlockSpec((1, L, nqk), lambda b, c: (b, c, COL_QK // nqk)),
            pl.BlockSpec((1, L, MLSTM_V), lambda b, c: (b, c, COL_V // MLSTM_V)),
            pl.BlockSpec((1, L, MLSTM_V), lambda b, c: (b, c, COL_O // MLSTM_V)),
            pl.BlockSpec((1, L, LANES), lambda b, c: (b, c, COL_IF // LANES)),
            pl.BlockSpec((MLSTM_CONV, nqk), lambda b, c: (0, 0)),
            pl.BlockSpec((SUBLANES, LANES), lambda b, c: (0, 0)),
        ],
        out_specs=pl.BlockSpec((1, L, MLSTM_V), lambda b, c: (b, c, 0)),
        out_shape=jax.ShapeDtypeStruct((B, T, MLSTM_V), BF16),
        scratch_shapes=[
            pltpu.VMEM((SUBLANES + L, nqk), F32),
            pltpu.VMEM((MLSTM_HEADS, MLSTM_DK, MLSTM_DK), F32),
            pltpu.VMEM((SUBLANES, MLSTM_DK), F32),
            pltpu.VMEM((SUBLANES, LANES), F32),
        ],
        compiler_params=_cparams(("parallel", "arbitrary")),
        name="mlstm",
    )(p, p, p, p, conv_w, bias)


def _head_sum_matrix(group):
    r = lax.broadcasted_iota(jnp.int32, (LANES, LANES), 0) // group
    c = lax.broadcasted_iota(jnp.int32, (LANES, LANES), 1) // group
    return jnp.where(r == c, 1.0, 0.0).astype(BF16)


def _rwkv_pre_kernel(p_ref, prev_ref, mu_ref, vec_ref, w2_ref, a2_ref, g2_ref,
                     r_ref, w_ref, k_ref, v_ref, aa_ref, bb_ref, g_ref):
    W = RWKV_WIDTH
    tb = p_ref.shape[1]
    p = p_ref[0]
    last = prev_ref[0, SUBLANES - 1:SUBLANES, :]
    last = jnp.where(pl.program_id(1) == 0, 0.0, last)
    rid = lax.broadcasted_iota(jnp.int32, (tb, 1), 0)
    shifted = jnp.where(rid == 0, last, pltpu.roll(p, 1, 0))
    p = p + (shifted - p) * mu_ref[...]

    r = p[:, 0:W]
    k = p[:, W:2 * W]
    v = p[:, 2 * W:3 * W]
    wl = p[:, 3 * W:3 * W + LORA_PAD]
    al = p[:, 3 * W + LORA_PAD:3 * W + 2 * LORA_PAD]
    gl = p[:, 3 * W + 2 * LORA_PAD:]
    w0 = vec_ref[0:1, :]
    a0 = vec_ref[1:2, :]
    k_k = vec_ref[2:3, :]
    k_a = vec_ref[3:4, :]

    wx = -(w0 + _dot(jnp.tanh(wl).astype(BF16), w2_ref[...]))
    softplus = jnp.maximum(wx, 0.0) + jnp.log1p(jnp.exp(-jnp.abs(wx)))
    w = -softplus - 0.5
    log_decay = -jnp.exp(w)
    a = jax.nn.sigmoid(a0 + _dot(al.astype(BF16), a2_ref[...]))
    g = _dot(jax.nn.sigmoid(gl).astype(BF16), g2_ref[...])

    kk = k * k_k
    ones = _head_sum_matrix(RWKV_HEAD)
    sq = kk * kk
    ss = jnp.concatenate(
        [_dot_exact_rhs(sq[:, j * LANES:(j + 1) * LANES], ones) for j in range(W // LANES)],
        axis=1)
    kk = kk / jnp.maximum(jnp.sqrt(ss), L2_EPS)

    r_ref[0] = r
    w_ref[0] = log_decay
    k_ref[0] = k * (1.0 + (a - 1.0) * k_a)
    v_ref[0] = v
    aa_ref[0] = -kk
    bb_ref[0] = kk * a
    g_ref[0] = g


def rwkv_pre(p, mu, w0, w2, a0, a2, g2, k_k, k_a, *, tb=256):
    B, T, _ = p.shape
    W = RWKV_WIDTH
    tb = min(tb, T)
    mu_p = jnp.zeros((1, RWKV_PACKED), F32)
    mu_p = mu_p.at[0, :3 * W].set(mu[:3 * W])
    mu_p = mu_p.at[0, 3 * W:3 * W + W_LORA].set(mu[3 * W:3 * W + W_LORA])
    mu_p = mu_p.at[0, 3 * W + LORA_PAD:3 * W + LORA_PAD + A_LORA].set(
        mu[3 * W + W_LORA:3 * W + W_LORA + A_LORA])
    mu_p = mu_p.at[0, 3 * W + 2 * LORA_PAD:].set(mu[3 * W + W_LORA + A_LORA:])
    vec = jnp.zeros((SUBLANES, W), F32)
    vec = vec.at[0].set(w0).at[1].set(a0).at[2].set(k_k).at[3].set(k_a)
    w2_p = jnp.zeros((LORA_PAD, W), F32).at[:W_LORA].set(w2).astype(BF16)
    a2_p = jnp.zeros((LORA_PAD, W), F32).at[:A_LORA].set(a2).astype(BF16)
    nprev = tb // SUBLANES
    out = jax.ShapeDtypeStruct((B, T, W), F32)
    ospec = pl.BlockSpec((1, tb, W), lambda b, i: (b, i, 0))
    return pl.pallas_call(
        _rwkv_pre_kernel,
        grid=(B, T // tb),
        in_specs=[
            pl.BlockSpec((1, tb, RWKV_PACKED), lambda b, i: (b, i, 0)),
            pl.BlockSpec((1, SUBLANES, RWKV_PACKED),
                         lambda b, i: (b, jnp.maximum(i * nprev - 1, 0), 0)),
            pl.BlockSpec((1, RWKV_PACKED), lambda b, i: (0, 0)),
            pl.BlockSpec((SUBLANES, W), lambda b, i: (0, 0)),
            pl.BlockSpec((LORA_PAD, W), lambda b, i: (0, 0)),
            pl.BlockSpec((LORA_PAD, W), lambda b, i: (0, 0)),
            pl.BlockSpec((G_LORA, W), lambda b, i: (0, 0)),
        ],
        out_specs=[ospec] * 7,
        out_shape=[out] * 7,
        compiler_params=_cparams(("parallel", "parallel")),
        name="rwkv_pre",
    )(p, p, mu_p, vec, w2_p, a2_p, g2.astype(BF16))


RWKV_CHUNK = 64


def _mm3(a, b, dims):
    ah, al = _split2(a)
    bh, bl = _split2(b)

    def dg(x, y):
        return lax.dot_general(x, y, (dims, ((), ())), preferred_element_type=F32)

    return dg(ah, bh) + dg(al, bh) + dg(ah, bl)


_NN = ((1,), (0,))
_NT = ((1,), (1,))
_TN = ((0,), (0,))


def _rwkv_chunk_kernel(r_ref, lw_ref, k_ref, v_ref, aa_ref, bb_ref, q_ref, y0_ref, p_ref, g_ref):
    L = RWKV_CHUNK
    N = RWKV_HEAD
    W = RWKV_WIDTH
    npair = W // LANES
    row = lax.broadcasted_iota(jnp.int32, (L, LANES), 0)
    lane = lax.broadcasted_iota(jnp.int32, (L, LANES), 1)
    first = lane < N
    pos = lane % N
    strict = pos < row
    lower = pos <= row
    eye = pos == row

    def bd(x):
        return jnp.concatenate([jnp.where(first, x, 0.0), jnp.where(first, 0.0, x)], axis=0)

    def unbd(x):
        return jnp.where(first, x[0:L], x[L:2 * L])

    tri = jnp.where(lax.broadcasted_iota(jnp.int32, (L, L), 1)
                    <= lax.broadcasted_iota(jnp.int32, (L, L), 0), 1.0, 0.0).astype(BF16)
    lw = lw_ref[0]
    c = _dot_exact_rhs_left(tri, lw)
    c_last = c[L - 1:L, :]
    e_pos = jnp.exp(c)
    e_neg = jnp.exp(-c)
    e_hat = jnp.exp(c_last - c)
    a_all = aa_ref[0]
    b_all = bb_ref[0]
    k_all = k_ref[0]
    at_all = a_all * jnp.exp(c - lw)
    rt_all = r_ref[0] * e_pos
    bt_all = b_all * e_neg
    kt_all = k_all * e_neg
    bh_all = b_all * e_hat
    kh_all = k_all * e_hat
    gl_all = jnp.exp(c_last)

    def pair(x, q):
        return x[:, q * LANES:(q + 1) * LANES]

    pairs = range(npair)
    at = [pair(at_all, q) for q in pairs]
    rt = [pair(rt_all, q) for q in pairs]
    v = [pair(v_ref[0], q) for q in pairs]
    bt_bd = [bd(pair(bt_all, q)) for q in pairs]
    kt_bd = [bd(pair(kt_all, q)) for q in pairs]
    v_bd = [bd(x) for x in v]

    aab = [jnp.where(strict, _mm3(at[q], bt_bd[q], _NT), 0.0) for q in pairs]
    aak = [jnp.where(strict, _mm3(at[q], kt_bd[q], _NT), 0.0) for q in pairs]
    mrb = [jnp.where(lower, _mm3(rt[q], bt_bd[q], _NT), 0.0) for q in pairs]
    mrk = [jnp.where(lower, _mm3(rt[q], kt_bd[q], _NT), 0.0) for q in pairs]

    x = aab
    t = [jnp.where(eye, 1.0, 0.0) + x[q] for q in pairs]
    n = 1
    while 2 * n < L:
        x_bd = [bd(x[q]) for q in pairs]
        x = [_mm3(x[q], x_bd[q], _NN) for q in pairs]
        x_bd = [bd(x[q]) for q in pairs]
        t = [t[q] + _mm3(t[q], x_bd[q], _NN) for q in pairs]
        n *= 2

    av = [_mm3(aak[q], v_bd[q], _NN) for q in pairs]
    w = [_mm3(t[q], bd(at[q]), _NN) for q in pairs]
    u0 = [_mm3(t[q], bd(av[q]), _NN) for q in pairs]
    w_bd = [bd(w[q]) for q in pairs]
    u0_bd = [bd(u0[q]) for q in pairs]
    qq = [rt[q] + _mm3(mrb[q], w_bd[q], _NN) for q in pairs]
    y0 = [_mm3(mrb[q], u0_bd[q], _NN) + _mm3(mrk[q], v_bd[q], _NN) for q in pairs]
    bh = [pair(bh_all, q) for q in pairs]
    kh = [pair(kh_all, q) for q in pairs]
    pp = [unbd(_mm3(bh[q], w[q], _TN)) + jnp.where(eye, pair(gl_all, q), 0.0) for q in pairs]
    gg = [unbd(_mm3(bh[q], u0[q], _TN)) + unbd(_mm3(kh[q], v[q], _TN)) for q in pairs]
    for q in pairs:
        sl = slice(q * LANES, (q + 1) * LANES)
        q_ref[0, :, sl] = qq[q]
        y0_ref[0, :, sl] = y0[q]
        p_ref[0, :, sl] = pp[q]
        g_ref[0, :, sl] = gg[q]


def _dot_exact_rhs_left(ones_bf16, x):
    hi, mid, lo = _split3(x)
    return _dot(ones_bf16, hi) + _dot(ones_bf16, mid) + _dot(ones_bf16, lo)


def _rwkv_state_kernel(q_ref, y0_ref, p_ref, g_ref, y_ref, h_ref):
    L = RWKV_CHUNK
    N = RWKV_HEAD
    npair = RWKV_WIDTH // LANES

    @pl.when(pl.program_id(1) == 0)
    def _():
        h_ref[...] = jnp.zeros_like(h_ref)

    first = lax.broadcasted_iota(jnp.int32, (N, LANES), 1) < N

    def bd(x):
        return jnp.concatenate([jnp.where(first, x, 0.0), jnp.where(first, 0.0, x)], axis=0)

    pairs = range(npair)
    sls = [slice(q * LANES, (q + 1) * LANES) for q in pairs]
    h_bd = [bd(h_ref[:, sls[q]]) for q in pairs]
    y = [_mm3(q_ref[0, :, sls[q]], h_bd[q], _NN) for q in pairs]
    hn = [_mm3(p_ref[0, :, sls[q]], h_bd[q], _NN) for q in pairs]
    for q in pairs:
        y_ref[0, :, sls[q]] = y[q] + y0_ref[0, :, sls[q]]
        h_ref[:, sls[q]] = hn[q] + g_ref[0, :, sls[q]]


def rwkv_chunked(r, lw, k, v, aa, bb):
    B, T, W = r.shape
    L = RWKV_CHUNK
    spec = pl.BlockSpec((1, L, W), lambda b, c: (b, c, 0))
    out = jax.ShapeDtypeStruct((B, T, W), F32)
    q, y0, p, g = pl.pallas_call(
        _rwkv_chunk_kernel,
        grid=(B, T // L),
        in_specs=[spec] * 6,
        out_specs=[spec] * 4,
        out_shape=[out] * 4,
        compiler_params=_cparams(("parallel", "parallel")),
        name="rwkv_chunk",
    )(r, lw, k, v, aa, bb)
    return pl.pallas_call(
        _rwkv_state_kernel,
        grid=(B, T // L),
        in_specs=[spec] * 4,
        out_specs=spec,
        out_shape=out,
        scratch_shapes=[pltpu.VMEM((RWKV_HEAD, W), F32)],
        compiler_params=_cparams(("parallel", "arbitrary")),
        name="rwkv_state",
    )(q, y0, p, g)


def _rwkv_post_kernel(y_ref, r_ref, k_ref, v_ref, g_ref, vec_ref, o_ref):
    W = RWKV_WIDTH
    ones = _head_sum_matrix(RWKV_HEAD)
    ln_w = vec_ref[0:1, :]
    ln_b = vec_ref[1:2, :]
    r_k = vec_ref[2:3, :]

    def head_sum(x):
        return jnp.concatenate(
            [_dot_exact_rhs(x[:, j * LANES:(j + 1) * LANES], ones) for j in range(W // LANES)],
            axis=1)

    y = y_ref[...]
    mean = head_sum(y) * (1.0 / RWKV_HEAD)
    d = y - mean
    var = head_sum(d * d) * (1.0 / RWKV_HEAD)
    yn = d * lax.rsqrt(var + GN_EPS) * ln_w + ln_b
    bonus = head_sum(r_ref[...] * k_ref[...] * r_k) * v_ref[...]
    o_ref[...] = ((yn + bonus) * g_ref[...]).astype(o_ref.dtype)


def rwkv_post(y, r, k, v, g, ln_w, ln_b, r_k, *, tb=256):
    M, W = y.shape
    tb = min(tb, M)
    vec = jnp.zeros((SUBLANES, W), F32)
    vec = vec.at[0].set(ln_w).at[1].set(ln_b).at[2].set(r_k.reshape(W))
    spec = pl.BlockSpec((tb, W), lambda i: (i, 0))
    return pl.pallas_call(
        _rwkv_post_kernel,
        grid=(M // tb,),
        in_specs=[spec] * 5 + [pl.BlockSpec((SUBLANES, W), lambda i: (0, 0))],
        out_specs=spec,
        out_shape=jax.ShapeDtypeStruct((M, W), BF16),
        compiler_params=_cparams(("parallel",)),
        name="rwkv_post",
    )(y, r, k, v, g, vec)


def _merge_kernel(ym_ref, yr_ref, gm_ref, gr_ref, pm_ref, pr_ref, o_ref):
    m = _dot(ym_ref[...], pm_ref[...])
    r = _dot(yr_ref[...], pr_ref[...])
    o_ref[...] = (jax.nn.sigmoid(gm_ref[...]) * m
                  + jax.nn.sigmoid(gr_ref[...]) * r).astype(o_ref.dtype)


def merge(y_m, y_r, gates, proj_m, proj_r, *, tm=512, tn=512):
    M, K = y_m.shape
    D = proj_m.shape[1]
    tm = min(tm, M)
    return pl.pallas_call(
        _merge_kernel,
        grid=(M // tm, D // tn),
        in_specs=[
            pl.BlockSpec((tm, K), lambda i, j: (i, 0)),
            pl.BlockSpec((tm, K), lambda i, j: (i, 0)),
            pl.BlockSpec((tm, tn), lambda i, j: (i, j)),
            pl.BlockSpec((tm, tn), lambda i, j: (i, D // tn + j)),
            pl.BlockSpec((K, tn), lambda i, j: (0, j)),
            pl.BlockSpec((K, tn), lambda i, j: (0, j)),
        ],
        out_specs=pl.BlockSpec((tm, tn), lambda i, j: (i, j)),
        out_shape=jax.ShapeDtypeStruct((M, D), BF16),
        compiler_params=_cparams(("parallel", "arbitrary")),
        name="merge",
    )(y_m, y_r, gates, gates, proj_m, proj_r)


def _out_proj_kernel(mix_ref, x_ref, w_ref, g_ref, h_ref, xn_ref):
    h = x_ref[...] + _dot(mix_ref[...], w_ref[...])
    h_ref[...] = h
    ms = jnp.mean(h * h, axis=-1, keepdims=True)
    xn_ref[...] = h * lax.rsqrt(ms + RMS_EPS) * g_ref[...]


def out_proj(mixed, x2d, w_out_bf16, gain, *, tm=256):
    M, D = x2d.shape
    tm = min(tm, M)
    spec = pl.BlockSpec((tm, D), lambda i: (i, 0))
    return pl.pallas_call(
        _out_proj_kernel,
        grid=(M // tm,),
        in_specs=[spec, spec, pl.BlockSpec((D, D), lambda i: (0, 0)),
                  pl.BlockSpec((1, D), lambda i: (0, 0))],
        out_specs=[spec, spec],
        out_shape=[jax.ShapeDtypeStruct((M, D), F32)] * 2,
        compiler_params=_cparams(("parallel",)),
        name="out_proj",
    )(mixed, x2d, w_out_bf16, gain.reshape(1, D))


def _matmul_split_kernel(a_ref, whi_ref, wlo_ref, o_ref, hi_ref, lo_ref):
    @pl.when(pl.program_id(1) == 0)
    def _():
        hi, lo = _split2(a_ref[...])
        hi_ref[...] = hi
        lo_ref[...] = lo

    o_ref[...] = (_dot(hi_ref[...], whi_ref[...]) + _dot(lo_ref[...], whi_ref[...])
                  + _dot(hi_ref[...], wlo_ref[...]))


def matmul_split(a, w, *, tm=512, tn=512):
    M, K = a.shape
    N = w.shape[1]
    tm = min(tm, M)
    whi = w.astype(BF16)
    wlo = (w - whi.astype(F32)).astype(BF16)
    return pl.pallas_call(
        _matmul_split_kernel,
        grid=(M // tm, N // tn),
        in_specs=[pl.BlockSpec((tm, K), lambda i, j: (i, 0)),
                  pl.BlockSpec((K, tn), lambda i, j: (0, j)),
                  pl.BlockSpec((K, tn), lambda i, j: (0, j))],
        out_specs=pl.BlockSpec((tm, tn), lambda i, j: (i, j)),
        out_shape=jax.ShapeDtypeStruct((M, N), F32),
        scratch_shapes=[pltpu.VMEM((tm, K), BF16), pltpu.VMEM((tm, K), BF16)],
        compiler_params=_cparams(("parallel", "arbitrary")),
        name="peer_query",
    )(a, whi, wlo)


def _route_kernel(q_ref, khi_ref, klo_ref, idx_ref, gate_ref,
                  s_ref, tv_ref, tp_ref, c_ref, e_ref, bv_ref, bi_ref):
    tn = q_ref.shape[0]
    KK = PEER_TOPK
    half = D_KEY // 2
    NEG = -jnp.inf
    key_id = lax.broadcasted_iota(jnp.int32, (N_KEYS, tn), 0).astype(F32)
    cand_id = lax.broadcasted_iota(jnp.int32, (KK * KK, tn), 0).astype(F32)
    rank = lax.broadcasted_iota(jnp.int32, (KK, tn), 0)

    def nt(a, b):
        return lax.dot_general(a, b, (((1,), (1,)), ((), ())), preferred_element_type=F32)

    for hp in range(2 * PEER_HEADS):
        qhi, qlo = _split2(q_ref[:, hp * half:(hp + 1) * half])
        p = hp % 2
        s_ref[hp] = nt(khi_ref[p], qhi) + nt(khi_ref[p], qlo) + nt(klo_ref[p], qhi)
    tv_ref[...] = jnp.zeros_like(tv_ref)
    tp_ref[...] = jnp.zeros_like(tp_ref)
    bv_ref[...] = jnp.zeros_like(bv_ref)
    bi_ref[...] = jnp.zeros_like(bi_ref)

    def sub_topk(kk, carry):
        for hp in range(2 * PEER_HEADS):
            s = s_ref[hp]
            m = jnp.max(s, axis=0, keepdims=True)
            pos = jnp.min(jnp.where(s == m, key_id, float(N_KEYS)), axis=0, keepdims=True)
            s_ref[hp] = jnp.where(key_id == pos, NEG, s)
            tv_ref[hp] = jnp.where(rank == kk, m, tv_ref[hp])
            tp_ref[hp] = jnp.where(rank == kk, pos, tp_ref[hp])
        return carry

    lax.fori_loop(0, KK, sub_topk, 0)

    for h in range(PEER_HEADS):
        v1, v2 = tv_ref[2 * h], tv_ref[2 * h + 1]
        p1, p2 = tp_ref[2 * h], tp_ref[2 * h + 1]
        for i in range(KK):
            c_ref[h, i * KK:(i + 1) * KK, :] = v1[i:i + 1, :] + v2
            e_ref[h, i * KK:(i + 1) * KK, :] = p1[i:i + 1, :] * float(N_KEYS) + p2

    def cand_topk(kk, carry):
        for h in range(PEER_HEADS):
            c = c_ref[h]
            m = jnp.max(c, axis=0, keepdims=True)
            pos = jnp.min(jnp.where(c == m, cand_id, float(KK * KK)), axis=0, keepdims=True)
            hit = cand_id == pos
            eid = jnp.max(jnp.where(hit, e_ref[h], -1.0), axis=0, keepdims=True)
            c_ref[h] = jnp.where(hit, NEG, c)
            rows = slice(h * KK, (h + 1) * KK)
            bv_ref[rows, :] = jnp.where(rank == kk, m, bv_ref[rows, :])
            bi_ref[rows, :] = jnp.where(rank == kk, eid, bi_ref[rows, :])
        return carry

    lax.fori_loop(0, KK, cand_topk, 0)

    for h in range(PEER_HEADS):
        rows = slice(h * KK, (h + 1) * KK)
        b = bv_ref[rows, :]
        e = jnp.exp(b - b[0:1, :])
        bv_ref[rows, :] = e / jnp.sum(e, axis=0, keepdims=True)
    gate_ref[...] = bv_ref[...].T
    idx_ref[...] = bi_ref[...].T.astype(jnp.int32)


def peer_route(q, sub_keys, *, tn=128):
    M = q.shape[0]
    tn = min(tn, M)
    khi = sub_keys.astype(BF16)
    klo = (sub_keys - khi.astype(F32)).astype(BF16)
    kspec = pl.BlockSpec((2, N_KEYS, D_KEY // 2), lambda i: (0, 0, 0))
    ospec = pl.BlockSpec((tn, N_KEYS), lambda i: (i, 0))
    KK2 = PEER_TOPK * PEER_TOPK
    return pl.pallas_call(
        _route_kernel,
        grid=(M // tn,),
        in_specs=[pl.BlockSpec((tn, PEER_HEADS * D_KEY), lambda i: (i, 0)), kspec, kspec],
        out_specs=[ospec, ospec],
        out_shape=[jax.ShapeDtypeStruct((M, N_KEYS), jnp.int32),
                   jax.ShapeDtypeStruct((M, N_KEYS), F32)],
        scratch_shapes=[
            pltpu.VMEM((2 * PEER_HEADS, N_KEYS, tn), F32),
            pltpu.VMEM((2 * PEER_HEADS, PEER_TOPK, tn), F32),
            pltpu.VMEM((2 * PEER_HEADS, PEER_TOPK, tn), F32),
            pltpu.VMEM((PEER_HEADS, KK2, tn), F32),
            pltpu.VMEM((PEER_HEADS, KK2, tn), F32),
            pltpu.VMEM((PEER_HEADS * PEER_TOPK, tn), F32),
            pltpu.VMEM((PEER_HEADS * PEER_TOPK, tn), F32),
        ],
        compiler_params=_cparams(("parallel",)),
        name="peer_route",
    )(q, khi, klo)


def _gelu_exact(x):
    return 0.5 * x * (1.0 + lax.erf(x * (2.0 ** -0.5)))


def _peer_expert_kernel(idx_ref, gate_ref, xn_ref, h_ref, gain_ref, uv_ref, o_ref,
                        buf_ref, acc_ref, sem):
    tb, D = xn_ref.shape
    E = idx_ref.shape[2]
    nw = D // (2 * LANES)
    step = pl.program_id(0)

    def slot_copy(slot):
        return pltpu.make_async_copy(buf_ref.at[slot], buf_ref.at[slot], sem.at[slot])

    def unpack(words):
        lo = lax.bitcast_convert_type(words << 16, F32)
        hi = lax.bitcast_convert_type(words & jnp.uint32(0xFFFF0000), F32)
        return lo, hi

    def issue(t, slot):
        for e in range(E):
            pltpu.make_async_copy(uv_ref.at[idx_ref[0, t, e]], buf_ref.at[slot, :, e, :],
                                  sem.at[slot]).start(priority=e % 2)

    @pl.when(step == 0)
    def _():
        for t in range(PEER_LOOKAHEAD):
            issue(t, t)

    eye = (lax.broadcasted_iota(jnp.int32, (E, E), 0)
           == lax.broadcasted_iota(jnp.int32, (E, E), 1))

    def body(t8, carry):
        base = pl.multiple_of(t8 * SUBLANES, SUBLANES)
        x8 = xn_ref[pl.ds(base, SUBLANES), :]
        g8 = gate_ref[pl.ds(base, SUBLANES), :]
        rows = []
        for j in range(SUBLANES):
            slot = j % PEER_SLOTS
            issue(base + j + PEER_LOOKAHEAD, (j + PEER_LOOKAHEAD) % PEER_SLOTS)
            slot_copy(slot).wait()
            part = jnp.zeros((E, LANES), F32)
            for s in range(nw):
                lo, hi = unpack(buf_ref[slot, s])
                part = (part + lo * x8[j:j + 1, s * LANES:(s + 1) * LANES]
                        + hi * x8[j:j + 1, (nw + s) * LANES:(nw + s + 1) * LANES])
            act = jnp.sum(part, axis=1, keepdims=True)
            g_col = jnp.sum(jnp.where(eye, g8[j:j + 1], 0.0), axis=1, keepdims=True)
            c = g_col * _gelu_exact(act)
            out_lo, out_hi = [], []
            for s in range(nw):
                lo, hi = unpack(buf_ref[slot, nw + s])
                out_lo.append(jnp.sum(lo * c, axis=0, keepdims=True))
                out_hi.append(jnp.sum(hi * c, axis=0, keepdims=True))
            rows.append(jnp.concatenate(out_lo + out_hi, axis=1))
        acc_ref[pl.ds(base, SUBLANES), :] = jnp.concatenate(rows, axis=0)
        return carry

    lax.fori_loop(0, tb // SUBLANES, body, 0)

    @pl.when(step == pl.num_programs(0) - 1)
    def _():
        for t in range(PEER_LOOKAHEAD):
            slot_copy(t % PEER_SLOTS).wait()

    hh = h_ref[...] + acc_ref[...]
    ms = jnp.mean(hh * hh, axis=-1, keepdims=True)
    o_ref[...] = hh * lax.rsqrt(ms + RMS_EPS) * gain_ref[...]


def pack_expert_rows(u, v):
    def pack(t):
        n, d = t.shape
        bits = lax.bitcast_convert_type(t.astype(BF16), jnp.uint16).astype(jnp.uint32)
        bits = bits.reshape(n, 2, d // (2 * LANES), LANES)
        return bits[:, 0] | (bits[:, 1] << 16)

    return jnp.concatenate([pack(u), pack(v)], axis=1)


def peer_experts(idx, gate, xn, h, gain, uv, *, tb=64):
    M, D = xn.shape
    E = idx.shape[1]
    tb = min(tb, M)
    assert tb % PEER_SLOTS == 0 and M % tb == 0
    nb = M // tb
    idx3 = idx.reshape(nb, tb, E)
    ahead = jnp.concatenate([idx3[1:, :PEER_SLOTS], idx3[-1:, :PEER_SLOTS]], axis=0)
    idx_ext = jnp.concatenate([idx3, ahead], axis=1)
    spec = pl.BlockSpec((tb, D), lambda i: (i, 0))
    return pl.pallas_call(
        _peer_expert_kernel,
        grid=(nb,),
        in_specs=[
            pl.BlockSpec((1, tb + PEER_SLOTS, E), lambda i: (i, 0, 0), memory_space=pltpu.SMEM),
            pl.BlockSpec((tb, E), lambda i: (i, 0)),
            spec, spec,
            pl.BlockSpec((1, D), lambda i: (0, 0)),
            pl.BlockSpec(memory_space=pl.ANY),
        ],
        out_specs=spec,
        out_shape=jax.ShapeDtypeStruct((M, D), F32),
        scratch_shapes=[
            pltpu.VMEM((PEER_SLOTS, D // LANES, E, LANES), jnp.uint32),
            pltpu.VMEM((tb, D), F32),
            pltpu.SemaphoreType.DMA((PEER_SLOTS,)),
        ],
        compiler_params=_cparams(("arbitrary",)),
        name="peer_experts",
    )(idx_ext, gate, xn, h, gain.reshape(1, D), uv)


def _pack_w_in(w_in):
    D = w_in.shape[0]
    W = RWKV_WIDTH
    o = 0
    qk = w_in[:, o:o + 2 * MLSTM_QK]; o += 2 * MLSTM_QK
    v = w_in[:, o:o + MLSTM_V]; o += MLSTM_V
    og = w_in[:, o:o + MLSTM_V]; o += MLSTM_V
    ifg = w_in[:, o:o + 2 * MLSTM_HEADS]; o += 2 * MLSTM_HEADS
    rkv = w_in[:, o:o + 3 * W]; o += 3 * W
    wl = w_in[:, o:o + W_LORA]; o += W_LORA
    al = w_in[:, o:o + A_LORA]; o += A_LORA
    gl = w_in[:, o:o + G_LORA]; o += G_LORA
    gates = w_in[:, o:]

    def pad(t, n):
        return jnp.pad(t, ((0, 0), (0, n - t.shape[1])))

    w_mlstm = jnp.concatenate([qk, v, og, pad(ifg, LANES)], axis=1)
    w_rwkv = jnp.concatenate([rkv, pad(wl, LORA_PAD), pad(al, LORA_PAD), gl], axis=1)
    assert w_mlstm.shape == (D, MLSTM_PACKED) and w_rwkv.shape == (D, RWKV_PACKED)
    return w_mlstm.astype(BF16), w_rwkv.astype(BF16), gates.astype(BF16)


def kernel(x, norm_mix_gain, w_in, mlstm_conv, mlstm_b_i, mlstm_b_f, rwkv_mu, rwkv_w0, rwkv_w2,
           rwkv_a0, rwkv_a2, rwkv_g2, rwkv_k_k, rwkv_k_a, rwkv_r_k, rwkv_ln_w, rwkv_ln_b,
           proj_mlstm, proj_rwkv, w_out, norm_ffn_gain, peer_w_query, peer_sub_keys,
           peer_u, peer_v, norm_final_gain):
    B, T, D = x.shape
    assert w_in.shape[0] == 1, "the output norm is fused into the single layer's PEER kernel"
    l = 0
    x2d = x.reshape(B * T, D)
    w_mlstm, w_rwkv, w_gates = _pack_w_in(w_in[l])
    p_m = norm_matmul(x2d, norm_mix_gain[l], w_mlstm, tn=MLSTM_PACKED // 3).reshape(B, T, MLSTM_PACKED)
    p_r = norm_matmul(x2d, norm_mix_gain[l], w_rwkv, tn=RWKV_PACKED // 4).reshape(B, T, RWKV_PACKED)
    p_g = norm_matmul(x2d, norm_mix_gain[l], w_gates, tn=1024)

    y_m = mlstm_branch(p_m, mlstm_conv[l], mlstm_b_i[l], mlstm_b_f[l])
    r, w, k, v, aa, bb, g = rwkv_pre(p_r, rwkv_mu[l], rwkv_w0[l], rwkv_w2[l], rwkv_a0[l],
                                     rwkv_a2[l], rwkv_g2[l], rwkv_k_k[l], rwkv_k_a[l])
    y = rwkv_chunked(r, w, k, v, aa, bb)

    def flat(t):
        return t.reshape(B * T, RWKV_WIDTH)

    y_r = rwkv_post(flat(y), flat(r), flat(k), flat(v), flat(g),
                    rwkv_ln_w[l], rwkv_ln_b[l], rwkv_r_k[l])
    mixed = merge(y_m.reshape(B * T, MLSTM_V), y_r, p_g,
                  proj_mlstm[l].astype(BF16), proj_rwkv[l].astype(BF16))
    h2d, xn = out_proj(mixed, x2d, w_out[l].astype(BF16), norm_ffn_gain[l])
    q = matmul_split(xn, peer_w_query[l])
    idx, gate = peer_route(q, peer_sub_keys[l])
    out = peer_experts(idx, gate, xn, h2d, norm_final_gain, pack_expert_rows(peer_u[l], peer_v[l]))
    return out.reshape(B, T, D)
```

```python
import jax
import jax.numpy as jnp
from jax import lax
from jax.experimental import pallas as pl
from jax.experimental.pallas import tpu as pltpu

F32 = jnp.float32
BF16 = jnp.bfloat16

LANES = 128
SUBLANES = 8
VMEM_LIMIT_BYTES = 48 * 1024 * 1024

MLSTM_HEADS = 4
MLSTM_DK = 256
MLSTM_CONV = 4
MLSTM_CHUNK = 64
RWKV_HEADS = 16
RWKV_HEAD = 64
RWKV_WIDTH = RWKV_HEADS * RWKV_HEAD
W_LORA = 96
A_LORA = 96
G_LORA = 256
PEER_HEADS = 8
N_KEYS = 128
PEER_TOPK = 16
D_KEY = 256
RMS_EPS = 1e-6
GN_EPS = 64e-5
L2_EPS = 1e-12

PEER_SLOTS = 8
PEER_LOOKAHEAD = 6

MLSTM_QK = MLSTM_HEADS * MLSTM_DK
MLSTM_V = MLSTM_QK
LORA_PAD = 128

COL_QK = 0
COL_V = COL_QK + 2 * MLSTM_QK
COL_O = COL_V + MLSTM_V
COL_IF = COL_O + MLSTM_V
MLSTM_PACKED = COL_IF + LANES
RWKV_PACKED = 3 * RWKV_WIDTH + 2 * LORA_PAD + G_LORA


def _cparams(sem):
    return pltpu.CompilerParams(dimension_semantics=sem, vmem_limit_bytes=VMEM_LIMIT_BYTES)


def _split2(x):
    hi = x.astype(BF16)
    lo = (x - hi.astype(F32)).astype(BF16)
    return hi, lo


def _split3(x):
    hi = x.astype(BF16)
    r1 = x - hi.astype(F32)
    mid = r1.astype(BF16)
    lo = (r1 - mid.astype(F32)).astype(BF16)
    return hi, mid, lo


def _dot(a, b):
    return jnp.dot(a, b, preferred_element_type=F32)


def _dot_exact_rhs(x, ones_bf16):
    hi, mid, lo = _split3(x)
    return _dot(hi, ones_bf16) + _dot(mid, ones_bf16) + _dot(lo, ones_bf16)


def _norm_matmul_kernel(x_ref, g_ref, w_ref, o_ref, xn_ref):
    @pl.when(pl.program_id(1) == 0)
    def _():
        x = x_ref[...]
        ms = jnp.mean(x * x, axis=-1, keepdims=True)
        xn_ref[...] = (x * lax.rsqrt(ms + RMS_EPS) * g_ref[...]).astype(BF16)

    o_ref[...] = _dot(xn_ref[...], w_ref[...])


def norm_matmul(x, gain, w_bf16, *, tm=512, tn=384):
    M, K = x.shape
    N = w_bf16.shape[1]
    tm = min(tm, M)
    return pl.pallas_call(
        _norm_matmul_kernel,
        grid=(M // tm, N // tn),
        in_specs=[
            pl.BlockSpec((tm, K), lambda i, j: (i, 0)),
            pl.BlockSpec((1, K), lambda i, j: (0, 0)),
            pl.BlockSpec((K, tn), lambda i, j: (0, j)),
        ],
        out_specs=pl.BlockSpec((tm, tn), lambda i, j: (i, j)),
        out_shape=jax.ShapeDtypeStruct((M, N), F32),
        scratch_shapes=[pltpu.VMEM((tm, K), BF16)],
        compiler_params=_cparams(("parallel", "arbitrary")),
        name="norm_matmul",
    )(x, gain.reshape(1, K), w_bf16)


def _mlstm_kernel(qk_ref, v_ref, o_ref, if_ref, conv_ref, bias_ref, y_ref,
                  ext_ref, c_ref, n_ref, m_ref):
    L = MLSTM_CHUNK
    DK = MLSTM_DK
    step = pl.program_id(1)

    @pl.when(step == 0)
    def _():
        ext_ref[0:SUBLANES, :] = jnp.zeros((SUBLANES, 2 * MLSTM_QK), F32)
        c_ref[...] = jnp.zeros_like(c_ref)
        n_ref[...] = jnp.zeros_like(n_ref)
        m_ref[...] = jnp.zeros_like(m_ref)

    ext_ref[SUBLANES:SUBLANES + L, :] = qk_ref[0]
    acc = jnp.zeros((L, 2 * MLSTM_QK), F32)
    for j in range(MLSTM_CONV):
        off = SUBLANES - (MLSTM_CONV - 1) + j
        acc = acc + ext_ref[off:off + L, :] * conv_ref[j:j + 1, :]
    ext_ref[0:SUBLANES, :] = qk_ref[0, L - SUBLANES:L, :]
    qk = acc * jax.nn.sigmoid(acc)

    row = lax.broadcasted_iota(jnp.int32, (L, L), 0)
    col = lax.broadcasted_iota(jnp.int32, (L, L), 1)
    causal = col <= row
    eye = col == row

    def to_row(x_col):
        return jnp.sum(jnp.where(eye, x_col, 0.0), axis=0, keepdims=True)

    gates = if_ref[0]
    for h in range(MLSTM_HEADS):
        q = qk[:, h * DK:(h + 1) * DK] * (DK ** -0.5)
        k = qk[:, MLSTM_QK + h * DK:MLSTM_QK + (h + 1) * DK]
        v = v_ref[0, :, h * DK:(h + 1) * DK]
        ig_col = gates[:, h:h + 1] + bias_ref[0:1, h:h + 1]
        lf_col = jax.nn.log_sigmoid(
            gates[:, MLSTM_HEADS + h:MLSTM_HEADS + h + 1]
            + bias_ref[1:2, h:h + 1])
        lf_row = to_row(lf_col)
        ig_row = to_row(ig_col)
        b_col = jnp.sum(jnp.where(causal, lf_row, 0.0), axis=1, keepdims=True)
        b_row = to_row(b_col)
        b_last = b_col[L - 1:L, :]
        m_prev = m_ref[h:h + 1, 0:1]
        C = c_ref[h]
        n_row = n_ref[h:h + 1, :]

        dmat = jnp.where(causal, b_col - b_row + ig_row, -jnp.inf)
        inter = b_col + m_prev
        m_t = jnp.maximum(inter, jnp.max(dmat, axis=1, keepdims=True))
        qb = q.astype(BF16)
        kb = k.astype(BF16)
        vb = v.astype(BF16)
        s = lax.dot_general(qb, kb, (((1,), (1,)), ((), ())),
                            preferred_element_type=F32) * jnp.exp(dmat - m_t)
        w_inter = jnp.exp(inter - m_t)
        num = _dot(s.astype(BF16), vb) + w_inter * _dot(qb, C.astype(BF16))
        qn = jnp.sum(q * n_row, axis=1, keepdims=True)
        den = jnp.sum(s, axis=1, keepdims=True) + w_inter * qn
        hh = num / jnp.maximum(jnp.abs(den), jnp.exp(-m_t))
        o = o_ref[0, :, h * DK:(h + 1) * DK]
        y_ref[0, :, h * DK:(h + 1) * DK] = (jax.nn.sigmoid(o) * hh).astype(y_ref.dtype)

        g_end = b_last - b_col + ig_col
        m_new = jnp.maximum(b_last + m_prev, jnp.max(g_end, axis=0, keepdims=True))
        decay = jnp.exp(b_last + m_prev - m_new)
        ws = jnp.exp(g_end - m_new)
        kw = k * ws
        c_ref[h] = decay * C + lax.dot_general(
            kw.astype(BF16), vb, (((0,), (0,)), ((), ())), preferred_element_type=F32)
        n_ref[h:h + 1, :] = decay * n_row + jnp.sum(kw, axis=0, keepdims=True)
        m_ref[h:h + 1, :] = jnp.broadcast_to(m_new, (1, LANES))


def mlstm_branch(p, conv_w, b_i, b_f):
    B, T, _ = p.shape
    L = MLSTM_CHUNK
    bias = jnp.zeros((SUBLANES, LANES), F32)
    bias = bias.at[0, :MLSTM_HEADS].set(b_i).at[1, :MLSTM_HEADS].set(b_f)
    nqk = 2 * MLSTM_QK
    return pl.pallas_call(
        _mlstm_kernel,
        grid=(B, T // L),
        in_specs=[
            pl.BlockSpec((1, L, nqk), lambda b, c: (b, c, COL_QK // nqk)),
            pl.BlockSpec((1, L, MLSTM_V), lambda b, c: (b, c, COL_V // MLSTM_V)),
            pl.BlockSpec((1, L, MLSTM_V), lambda b, c: (b, c, COL_O // MLSTM_V)),
            pl.BlockSpec((1, L, LANES), lambda b, c: (b, c, COL_IF // LANES)),
            pl.BlockSpec((MLSTM_CONV, nqk), lambda b, c: (0, 0)),
            pl.BlockSpec((SUBLANES, LANES), lambda b, c: (0, 0)),
        ],
        out_specs=pl.BlockSpec((1, L, MLSTM_V), lambda b, c: (b, c, 0)),
        out_shape=jax.ShapeDtypeStruct((B, T, MLSTM_V), BF16),
        scratch_shapes=[
            pltpu.VMEM((SUBLANES + L, nqk), F32),
            pltpu.VMEM((MLSTM_HEADS, MLSTM_DK, MLSTM_DK), F32),
            pltpu.VMEM((SUBLANES, MLSTM_DK), F32),
            pltpu.VMEM((SUBLANES, LANES), F32),
        ],
        compiler_params=_cparams(("parallel", "arbitrary")),
        name="mlstm",
    )(p, p, p, p, conv_w, bias)


def _head_sum_matrix(group):
    r = lax.broadcasted_iota(jnp.int32, (LANES, LANES), 0) // group
    c = lax.broadcasted_iota(jnp.int32, (LANES, LANES), 1) // group
    return jnp.where(r == c, 1.0, 0.0).astype(BF16)


def _rwkv_pre_kernel(p_ref, prev_ref, mu_ref, vec_ref, w2_ref, a2_ref, g2_ref,
                     r_ref, w_ref, k_ref, v_ref, aa_ref, bb_ref, g_ref):
    W = RWKV_WIDTH
    tb = p_ref.shape[1]
    p = p_ref[0]
    last = prev_ref[0, SUBLANES - 1:SUBLANES, :]
    last = jnp.where(pl.program_id(1) == 0, 0.0, last)
    rid = lax.broadcasted_iota(jnp.int32, (tb, 1), 0)
    shifted = jnp.where(rid == 0, last, pltpu.roll(p, 1, 0))
    p = p + (shifted - p) * mu_ref[...]

    r = p[:, 0:W]
    k = p[:, W:2 * W]
    v = p[:, 2 * W:3 * W]
    wl = p[:, 3 * W:3 * W + LORA_PAD]
    al = p[:, 3 * W + LORA_PAD:3 * W + 2 * LORA_PAD]
    gl = p[:, 3 * W + 2 * LORA_PAD:]
    w0 = vec_ref[0:1, :]
    a0 = vec_ref[1:2, :]
    k_k = vec_ref[2:3, :]
    k_a = vec_ref[3:4, :]

    wx = -(w0 + _dot(jnp.tanh(wl).astype(BF16), w2_ref[...]))
    softplus = jnp.maximum(wx, 0.0) + jnp.log1p(jnp.exp(-jnp.abs(wx)))
    w = -softplus - 0.5
    log_decay = -jnp.exp(w)
    a = jax.nn.sigmoid(a0 + _dot(al.astype(BF16), a2_ref[...]))
    g = _dot(jax.nn.sigmoid(gl).astype(BF16), g2_ref[...])

    kk = k * k_k
    ones = _head_sum_matrix(RWKV_HEAD)
    sq = kk * kk
    ss = jnp.concatenate(
        [_dot_exact_rhs(sq[:, j * LANES:(j + 1) * LANES], ones) for j in range(W // LANES)],
        axis=1)
    kk = kk / jnp.maximum(jnp.sqrt(ss), L2_EPS)

    r_ref[0] = r
    w_ref[0] = log_decay
    k_ref[0] = k * (1.0 + (a - 1.0) * k_a)
    v_ref[0] = v
    aa_ref[0] = -kk
    bb_ref[0] = kk * a
    g_ref[0] = g


def rwkv_pre(p, mu, w0, w2, a0, a2, g2, k_k, k_a, *, tb=256):
    B, T, _ = p.shape
    W = RWKV_WIDTH
    tb = min(tb, T)
    mu_p = jnp.zeros((1, RWKV_PACKED), F32)
    mu_p = mu_p.at[0, :3 * W].set(mu[:3 * W])
    mu_p = mu_p.at[0, 3 * W:3 * W + W_LORA].set(mu[3 * W:3 * W + W_LORA])
    mu_p = mu_p.at[0, 3 * W + LORA_PAD:3 * W + LORA_PAD + A_LORA].set(
        mu[3 * W + W_LORA:3 * W + W_LORA + A_LORA])
    mu_p = mu_p.at[0, 3 * W + 2 * LORA_PAD:].set(mu[3 * W + W_LORA + A_LORA:])
    vec = jnp.zeros((SUBLANES, W), F32)
    vec = vec.at[0].set(w0).at[1].set(a0).at[2].set(k_k).at[3].set(k_a)
    w2_p = jnp.zeros((LORA_PAD, W), F32).at[:W_LORA].set(w2).astype(BF16)
    a2_p = jnp.zeros((LORA_PAD, W), F32).at[:A_LORA].set(a2).astype(BF16)
    nprev = tb // SUBLANES
    out = jax.ShapeDtypeStruct((B, T, W), F32)
    ospec = pl.BlockSpec((1, tb, W), lambda b, i: (b, i, 0))
    return pl.pallas_call(
        _rwkv_pre_kernel,
        grid=(B, T // tb),
        in_specs=[
            pl.BlockSpec((1, tb, RWKV_PACKED), lambda b, i: (b, i, 0)),
            pl.BlockSpec((1, SUBLANES, RWKV_PACKED),
                         lambda b, i: (b, jnp.maximum(i * nprev - 1, 0), 0)),
            pl.BlockSpec((1, RWKV_PACKED), lambda b, i: (0, 0)),
            pl.BlockSpec((SUBLANES, W), lambda b, i: (0, 0)),
            pl.BlockSpec((LORA_PAD, W), lambda b, i: (0, 0)),
            pl.BlockSpec((LORA_PAD, W), lambda b, i: (0, 0)),
            pl.BlockSpec((G_LORA, W), lambda b, i: (0, 0)),
        ],
        out_specs=[ospec] * 7,
        out_shape=[out] * 7,
        compiler_params=_cparams(("parallel", "parallel")),
        name="rwkv_pre",
    )(p, p, mu_p, vec, w2_p, a2_p, g2.astype(BF16))


RWKV_CHUNK = 64


def _mm3(a, b, dims):
    ah, al = _split2(a)
    bh, bl = _split2(b)

    def dg(x, y):
        return lax.dot_general(x, y, (dims, ((), ())), preferred_element_type=F32)

    return dg(ah, bh) + dg(al, bh) + dg(ah, bl)


_NN = ((1,), (0,))
_NT = ((1,), (1,))
_TN = ((0,), (0,))


def _rwkv_chunk_kernel(r_ref, lw_ref, k_ref, v_ref, aa_ref, bb_ref, q_ref, y0_ref, p_ref, g_ref):
    L = RWKV_CHUNK
    N = RWKV_HEAD
    W = RWKV_WIDTH
    npair = W // LANES
    row = lax.broadcasted_iota(jnp.int32, (L, LANES), 0)
    lane = lax.broadcasted_iota(jnp.int32, (L, LANES), 1)
    first = lane < N
    pos = lane % N
    strict = pos < row
    lower = pos <= row
    eye = pos == row

    def bd(x):
        return jnp.concatenate([jnp.where(first, x, 0.0), jnp.where(first, 0.0, x)], axis=0)

    def unbd(x):
        return jnp.where(first, x[0:L], x[L:2 * L])

    tri = jnp.where(lax.broadcasted_iota(jnp.int32, (L, L), 1)
                    <= lax.broadcasted_iota(jnp.int32, (L, L), 0), 1.0, 0.0).astype(BF16)
    lw = lw_ref[0]
    c = _dot_exact_rhs_left(tri, lw)
    c_last = c[L - 1:L, :]
    e_pos = jnp.exp(c)
    e_neg = jnp.exp(-c)
    e_hat = jnp.exp(c_last - c)
    a_all = aa_ref[0]
    b_all = bb_ref[0]
    k_all = k_ref[0]
    at_all = a_all * jnp.exp(c - lw)
    rt_all = r_ref[0] * e_pos
    bt_all = b_all * e_neg
    kt_all = k_all * e_neg
    bh_all = b_all * e_hat
    kh_all = k_all * e_hat
    gl_all = jnp.exp(c_last)

    def pair(x, q):
        return x[:, q * LANES:(q + 1) * LANES]

    pairs = range(npair)
    at = [pair(at_all, q) for q in pairs]
    rt = [pair(rt_all, q) for q in pairs]
    v = [pair(v_ref[0], q) for q in pairs]
    bt_bd = [bd(pair(bt_all, q)) for q in pairs]
    kt_bd = [bd(pair(kt_all, q)) for q in pairs]
    v_bd = [bd(x) for x in v]

    ar = [jnp.concatenate([at[q], rt[q]], axis=0) for q in pairs]
    sb = [_mm3(ar[q], bt_bd[q], _NT) for q in pairs]
    sk = [_mm3(ar[q], kt_bd[q], _NT) for q in pairs]
    aab = [jnp.where(strict, sb[q][0:L], 0.0) for q in pairs]
    mrb = [jnp.where(lower, sb[q][L:2 * L], 0.0) for q in pairs]
    aak_mrk = [jnp.concatenate([jnp.where(strict, sk[q][0:L], 0.0),
                                jnp.where(lower, sk[q][L:2 * L], 0.0)], axis=0) for q in pairs]

    t = [jnp.where(eye, 1.0, 0.0) + aab[q] for q in pairs]
    x = [_mm3(aab[q], bd(aab[q]), _NN) for q in pairs]
    n = 2
    while 2 * n < L:
        tx = [_mm3(jnp.concatenate([t[q], x[q]], axis=0), bd(x[q]), _NN) for q in pairs]
        t = [t[q] + tx[q][0:L] for q in pairs]
        x = [tx[q][L:2 * L] for q in pairs]
        n *= 2
    t = [t[q] + _mm3(t[q], bd(x[q]), _NN) for q in pairs]

    av_mv = [_mm3(aak_mrk[q], v_bd[q], _NN) for q in pairs]
    wu = [_mm3(t[q], jnp.concatenate([bd(at[q]), bd(av_mv[q][0:L])], axis=1), _NN)
          for q in pairs]
    w = [wu[q][:, 0:LANES] for q in pairs]
    u0 = [wu[q][:, LANES:2 * LANES] for q in pairs]
    qy = [_mm3(mrb[q], jnp.concatenate([bd(w[q]), bd(u0[q])], axis=1), _NN) for q in pairs]
    qq = [rt[q] + qy[q][:, 0:LANES] for q in pairs]
    y0 = [qy[q][:, LANES:2 * LANES] + av_mv[q][L:2 * L] for q in pairs]
    bh = [pair(bh_all, q) for q in pairs]
    kh = [pair(kh_all, q) for q in pairs]
    pg = [_mm3(bh[q], wu[q], _TN) for q in pairs]
    kv = [_mm3(kh[q], v[q], _TN) for q in pairs]
    pp = [unbd(pg[q][:, 0:LANES]) + jnp.where(eye, pair(gl_all, q), 0.0) for q in pairs]
    gg = [unbd(pg[q][:, LANES:2 * LANES]) + unbd(kv[q]) for q in pairs]
    for q in pairs:
        sl = slice(q * LANES, (q + 1) * LANES)
        q_ref[0, :, sl] = qq[q]
        y0_ref[0, :, sl] = y0[q]
        p_ref[0, :, sl] = pp[q]
        g_ref[0, :, sl] = gg[q]


def _dot_exact_rhs_left(ones_bf16, x):
    hi, mid, lo = _split3(x)
    return _dot(ones_bf16, hi) + _dot(ones_bf16, mid) + _dot(ones_bf16, lo)


def _rwkv_state_kernel(q_ref, y0_ref, p_ref, g_ref, y_ref, h_ref):
    L = RWKV_CHUNK
    N = RWKV_HEAD
    npair = RWKV_WIDTH // LANES

    @pl.when(pl.program_id(1) == 0)
    def _():
        h_ref[...] = jnp.zeros_like(h_ref)

    first = lax.broadcasted_iota(jnp.int32, (N, LANES), 1) < N

    def bd(x):
        return jnp.concatenate([jnp.where(first, x, 0.0), jnp.where(first, 0.0, x)], axis=0)

    pairs = range(npair)
    sls = [slice(q * LANES, (q + 1) * LANES) for q in pairs]
    h_bd = [bd(h_ref[:, sls[q]]) for q in pairs]
    y = [_mm3(q_ref[0, :, sls[q]], h_bd[q], _NN) for q in pairs]
    hn = [_mm3(p_ref[0, :, sls[q]], h_bd[q], _NN) for q in pairs]
    for q in pairs:
        y_ref[0, :, sls[q]] = y[q] + y0_ref[0, :, sls[q]]
        h_ref[:, sls[q]] = hn[q] + g_ref[0, :, sls[q]]


def rwkv_chunked(r, lw, k, v, aa, bb):
    B, T, W = r.shape
    L = RWKV_CHUNK
    spec = pl.BlockSpec((1, L, W), lambda b, c: (b, c, 0))
    out = jax.ShapeDtypeStruct((B, T, W), F32)
    q, y0, p, g = pl.pallas_call(
        _rwkv_chunk_kernel,
        grid=(B, T // L),
        in_specs=[spec] * 6,
        out_specs=[spec] * 4,
        out_shape=[out] * 4,
        compiler_params=_cparams(("parallel", "parallel")),
        name="rwkv_chunk",
    )(r, lw, k, v, aa, bb)
    return pl.pallas_call(
        _rwkv_state_kernel,
        grid=(B, T // L),
        in_specs=[spec] * 4,
        out_specs=spec,
        out_shape=out,
        scratch_shapes=[pltpu.VMEM((RWKV_HEAD, W), F32)],
        compiler_params=_cparams(("parallel", "arbitrary")),
        name="rwkv_state",
    )(q, y0, p, g)


def _rwkv_post_kernel(y_ref, r_ref, k_ref, v_ref, g_ref, vec_ref, o_ref):
    W = RWKV_WIDTH
    ones = _head_sum_matrix(RWKV_HEAD)
    ln_w = vec_ref[0:1, :]
    ln_b = vec_ref[1:2, :]
    r_k = vec_ref[2:3, :]

    def head_sum(x):
        return jnp.concatenate(
            [_dot_exact_rhs(x[:, j * LANES:(j + 1) * LANES], ones) for j in range(W // LANES)],
            axis=1)

    y = y_ref[...]
    mean = head_sum(y) * (1.0 / RWKV_HEAD)
    d = y - mean
    var = head_sum(d * d) * (1.0 / RWKV_HEAD)
    yn = d * lax.rsqrt(var + GN_EPS) * ln_w + ln_b
    bonus = head_sum(r_ref[...] * k_ref[...] * r_k) * v_ref[...]
    o_ref[...] = ((yn + bonus) * g_ref[...]).astype(o_ref.dtype)


def rwkv_post(y, r, k, v, g, ln_w, ln_b, r_k, *, tb=256):
    M, W = y.shape
    tb = min(tb, M)
    vec = jnp.zeros((SUBLANES, W), F32)
    vec = vec.at[0].set(ln_w).at[1].set(ln_b).at[2].set(r_k.reshape(W))
    spec = pl.BlockSpec((tb, W), lambda i: (i, 0))
    return pl.pallas_call(
        _rwkv_post_kernel,
        grid=(M // tb,),
        in_specs=[spec] * 5 + [pl.BlockSpec((SUBLANES, W), lambda i: (0, 0))],
        out_specs=spec,
        out_shape=jax.ShapeDtypeStruct((M, W), BF16),
        compiler_params=_cparams(("parallel",)),
        name="rwkv_post",
    )(y, r, k, v, g, vec)


def _merge_kernel(ym_ref, yr_ref, gm_ref, gr_ref, pm_ref, pr_ref, o_ref):
    m = _dot(ym_ref[...], pm_ref[...])
    r = _dot(yr_ref[...], pr_ref[...])
    o_ref[...] = (jax.nn.sigmoid(gm_ref[...]) * m
                  + jax.nn.sigmoid(gr_ref[...]) * r).astype(o_ref.dtype)


def merge(y_m, y_r, gates, proj_m, proj_r, *, tm=512, tn=512):
    M, K = y_m.shape
    D = proj_m.shape[1]
    tm = min(tm, M)
    return pl.pallas_call(
        _merge_kernel,
        grid=(M // tm, D // tn),
        in_specs=[
            pl.BlockSpec((tm, K), lambda i, j: (i, 0)),
            pl.BlockSpec((tm, K), lambda i, j: (i, 0)),
            pl.BlockSpec((tm, tn), lambda i, j: (i, j)),
            pl.BlockSpec((tm, tn), lambda i, j: (i, D // tn + j)),
            pl.BlockSpec((K, tn), lambda i, j: (0, j)),
            pl.BlockSpec((K, tn), lambda i, j: (0, j)),
        ],
        out_specs=pl.BlockSpec((tm, tn), lambda i, j: (i, j)),
        out_shape=jax.ShapeDtypeStruct((M, D), BF16),
        compiler_params=_cparams(("parallel", "arbitrary")),
        name="merge",
    )(y_m, y_r, gates, gates, proj_m, proj_r)


def _out_proj_kernel(mix_ref, x_ref, w_ref, g_ref, h_ref, xn_ref):
    h = x_ref[...] + _dot(mix_ref[...], w_ref[...])
    h_ref[...] = h
    ms = jnp.mean(h * h, axis=-1, keepdims=True)
    xn_ref[...] = h * lax.rsqrt(ms + RMS_EPS) * g_ref[...]


def out_proj(mixed, x2d, w_out_bf16, gain, *, tm=256):
    M, D = x2d.shape
    tm = min(tm, M)
    spec = pl.BlockSpec((tm, D), lambda i: (i, 0))
    return pl.pallas_call(
        _out_proj_kernel,
        grid=(M // tm,),
        in_specs=[spec, spec, pl.BlockSpec((D, D), lambda i: (0, 0)),
                  pl.BlockSpec((1, D), lambda i: (0, 0))],
        out_specs=[spec, spec],
        out_shape=[jax.ShapeDtypeStruct((M, D), F32)] * 2,
        compiler_params=_cparams(("parallel",)),
        name="out_proj",
    )(mixed, x2d, w_out_bf16, gain.reshape(1, D))


def _matmul_split_kernel(a_ref, whi_ref, wlo_ref, o_ref, hi_ref, lo_ref):
    @pl.when(pl.program_id(1) == 0)
    def _():
        hi, lo = _split2(a_ref[...])
        hi_ref[...] = hi
        lo_ref[...] = lo

    o_ref[...] = (_dot(hi_ref[...], whi_ref[...]) + _dot(lo_ref[...], whi_ref[...])
                  + _dot(hi_ref[...], wlo_ref[...]))


def matmul_split(a, w, *, tm=512, tn=512):
    M, K = a.shape
    N = w.shape[1]
    tm = min(tm, M)
    whi = w.astype(BF16)
    wlo = (w - whi.astype(F32)).astype(BF16)
    return pl.pallas_call(
        _matmul_split_kernel,
        grid=(M // tm, N // tn),
        in_specs=[pl.BlockSpec((tm, K), lambda i, j: (i, 0)),
                  pl.BlockSpec((K, tn), lambda i, j: (0, j)),
                  pl.BlockSpec((K, tn), lambda i, j: (0, j))],
        out_specs=pl.BlockSpec((tm, tn), lambda i, j: (i, j)),
        out_shape=jax.ShapeDtypeStruct((M, N), F32),
        scratch_shapes=[pltpu.VMEM((tm, K), BF16), pltpu.VMEM((tm, K), BF16)],
        compiler_params=_cparams(("parallel", "arbitrary")),
        name="peer_query",
    )(a, whi, wlo)


def _route_kernel(q_ref, khi_ref, klo_ref, idx_ref, gate_ref,
                  s_ref, tv_ref, tp_ref, c_ref, e_ref, bv_ref, bi_ref):
    tn = q_ref.shape[0]
    KK = PEER_TOPK
    half = D_KEY // 2
    NEG = -jnp.inf
    key_id = lax.broadcasted_iota(jnp.int32, (N_KEYS, tn), 0).astype(F32)
    cand_id = lax.broadcasted_iota(jnp.int32, (KK * KK, tn), 0).astype(F32)
    rank = lax.broadcasted_iota(jnp.int32, (KK, tn), 0)

    def nt(a, b):
        return lax.dot_general(a, b, (((1,), (1,)), ((), ())), preferred_element_type=F32)

    for hp in range(2 * PEER_HEADS):
        qhi, qlo = _split2(q_ref[:, hp * half:(hp + 1) * half])
        p = hp % 2
        s_ref[hp] = nt(khi_ref[p], qhi) + nt(khi_ref[p], qlo) + nt(klo_ref[p], qhi)
    tv_ref[...] = jnp.zeros_like(tv_ref)
    tp_ref[...] = jnp.zeros_like(tp_ref)
    bv_ref[...] = jnp.zeros_like(bv_ref)
    bi_ref[...] = jnp.zeros_like(bi_ref)

    def sub_topk(kk, carry):
        for hp in range(2 * PEER_HEADS):
            s = s_ref[hp]
            m = jnp.max(s, axis=0, keepdims=True)
            pos = jnp.min(jnp.where(s == m, key_id, float(N_KEYS)), axis=0, keepdims=True)
            s_ref[hp] = jnp.where(key_id == pos, NEG, s)
            tv_ref[hp] = jnp.where(rank == kk, m, tv_ref[hp])
            tp_ref[hp] = jnp.where(rank == kk, pos, tp_ref[hp])
        return carry

    lax.fori_loop(0, KK, sub_topk, 0)

    for h in range(PEER_HEADS):
        v1, v2 = tv_ref[2 * h], tv_ref[2 * h + 1]
        p1, p2 = tp_ref[2 * h], tp_ref[2 * h + 1]
        for i in range(KK):
            c_ref[h, i * KK:(i + 1) * KK, :] = v1[i:i + 1, :] + v2
            e_ref[h, i * KK:(i + 1) * KK, :] = p1[i:i + 1, :] * float(N_KEYS) + p2

    def cand_topk(kk, carry):
        for h in range(PEER_HEADS):
            c = c_ref[h]
            m = jnp.max(c, axis=0, keepdims=True)
            pos = jnp.min(jnp.where(c == m, cand_id, float(KK * KK)), axis=0, keepdims=True)
            hit = cand_id == pos
            eid = jnp.max(jnp.where(hit, e_ref[h], -1.0), axis=0, keepdims=True)
            c_ref[h] = jnp.where(hit, NEG, c)
            rows = slice(h * KK, (h + 1) * KK)
            bv_ref[rows, :] = jnp.where(rank == kk, m, bv_ref[rows, :])
            bi_ref[rows, :] = jnp.where(rank == kk, eid, bi_ref[rows, :])
        return carry

    lax.fori_loop(0, KK, cand_topk, 0)

    for h in range(PEER_HEADS):
        rows = slice(h * KK, (h + 1) * KK)
        b = bv_ref[rows, :]
        e = jnp.exp(b - b[0:1, :])
        bv_ref[rows, :] = e / jnp.sum(e, axis=0, keepdims=True)
    gate_ref[...] = bv_ref[...].T
    idx_ref[...] = bi_ref[...].T.astype(jnp.int32)


def peer_route(q, sub_keys, *, tn=128):
    M = q.shape[0]
    tn = min(tn, M)
    khi = sub_keys.astype(BF16)
    klo = (sub_keys - khi.astype(F32)).astype(BF16)
    kspec = pl.BlockSpec((2, N_KEYS, D_KEY // 2), lambda i: (0, 0, 0))
    ospec = pl.BlockSpec((tn, N_KEYS), lambda i: (i, 0))
    KK2 = PEER_TOPK * PEER_TOPK
    return pl.pallas_call(
        _route_kernel,
        grid=(M // tn,),
        in_specs=[pl.BlockSpec((tn, PEER_HEADS * D_KEY), lambda i: (i, 0)), kspec, kspec],
        out_specs=[ospec, ospec],
        out_shape=[jax.ShapeDtypeStruct((M, N_KEYS), jnp.int32),
                   jax.ShapeDtypeStruct((M, N_KEYS), F32)],
        scratch_shapes=[
            pltpu.VMEM((2 * PEER_HEADS, N_KEYS, tn), F32),
            pltpu.VMEM((2 * PEER_HEADS, PEER_TOPK, tn), F32),
            pltpu.VMEM((2 * PEER_HEADS, PEER_TOPK, tn), F32),
            pltpu.VMEM((PEER_HEADS, KK2, tn), F32),
            pltpu.VMEM((PEER_HEADS, KK2, tn), F32),
            pltpu.VMEM((PEER_HEADS * PEER_TOPK, tn), F32),
            pltpu.VMEM((PEER_HEADS * PEER_TOPK, tn), F32),
        ],
        compiler_params=_cparams(("parallel",)),
        name="peer_route",
    )(q, khi, klo)


def _gelu_exact(x):
    return 0.5 * x * (1.0 + lax.erf(x * (2.0 ** -0.5)))


def _peer_expert_kernel(idx_ref, gate_ref, xn_ref, h_ref, gain_ref, uv_ref, o_ref,
                        buf_even_ref, buf_odd_ref, acc_ref, sem):
    tb, D = xn_ref.shape
    E = idx_ref.shape[2]
    nw = D // (2 * LANES)
    step = pl.program_id(0)

    def staging(slot):
        return (buf_odd_ref if slot % 2 else buf_even_ref).at[slot // 2]

    def slot_copy(slot):
        return pltpu.make_async_copy(staging(slot), staging(slot), sem.at[slot])

    def unpack(words):
        lo = lax.bitcast_convert_type(words << 16, F32)
        hi = lax.bitcast_convert_type(words & jnp.uint32(0xFFFF0000), F32)
        return lo, hi

    def issue(t, slot):
        for e in range(E):
            pltpu.make_async_copy(uv_ref.at[idx_ref[0, t, e]], staging(slot).at[:, e, :],
                                  sem.at[slot]).start(priority=e % 2)

    @pl.when(step == 0)
    def _():
        for t in range(PEER_LOOKAHEAD):
            issue(t, t)

    eye = (lax.broadcasted_iota(jnp.int32, (E, E), 0)
           == lax.broadcasted_iota(jnp.int32, (E, E), 1))

    def body(t8, carry):
        base = pl.multiple_of(t8 * SUBLANES, SUBLANES)
        x8 = xn_ref[pl.ds(base, SUBLANES), :]
        g8 = gate_ref[pl.ds(base, SUBLANES), :]
        rows = []
        for j in range(SUBLANES):
            slot = j % PEER_SLOTS
            issue(base + j + PEER_LOOKAHEAD, (j + PEER_LOOKAHEAD) % PEER_SLOTS)
            slot_copy(slot).wait()
            part = jnp.zeros((E, LANES), F32)
            for s in range(nw):
                lo, hi = unpack(staging(slot)[s])
                part = (part + lo * x8[j:j + 1, s * LANES:(s + 1) * LANES]
                        + hi * x8[j:j + 1, (nw + s) * LANES:(nw + s + 1) * LANES])
            act = jnp.sum(part, axis=1, keepdims=True)
            g_col = jnp.sum(jnp.where(eye, g8[j:j + 1], 0.0), axis=1, keepdims=True)
            c = g_col * _gelu_exact(act)
            out_lo, out_hi = [], []
            for s in range(nw):
                lo, hi = unpack(staging(slot)[nw + s])
                out_lo.append(jnp.sum(lo * c, axis=0, keepdims=True))
                out_hi.append(jnp.sum(hi * c, axis=0, keepdims=True))
            rows.append(jnp.concatenate(out_lo + out_hi, axis=1))
        acc_ref[pl.ds(base, SUBLANES), :] = jnp.concatenate(rows, axis=0)
        return carry

    lax.fori_loop(0, tb // SUBLANES, body, 0)

    @pl.when(step == pl.num_programs(0) - 1)
    def _():
        for t in range(PEER_LOOKAHEAD):
            slot_copy(t % PEER_SLOTS).wait()

    hh = h_ref[...] + acc_ref[...]
    ms = jnp.mean(hh * hh, axis=-1, keepdims=True)
    o_ref[...] = hh * lax.rsqrt(ms + RMS_EPS) * gain_ref[...]


def pack_expert_rows(u, v):
    n, d = u.shape
    nw = d // (2 * LANES)
    tn = 256

    def pack_kernel(u_ref, v_ref, o_ref):
        def pack(t):
            lo = lax.bitcast_convert_type(t[:, :d // 2].astype(BF16).astype(F32), jnp.uint32) >> 16
            hi = lax.bitcast_convert_type(t[:, d // 2:].astype(BF16).astype(F32), jnp.uint32)
            return lo | (hi & jnp.uint32(0xFFFF0000))

        wu = pack(u_ref[...])
        wv = pack(v_ref[...])
        for s in range(nw):
            o_ref[:, s, :] = wu[:, s * LANES:(s + 1) * LANES]
            o_ref[:, nw + s, :] = wv[:, s * LANES:(s + 1) * LANES]

    return pl.pallas_call(
        pack_kernel,
        grid=(n // tn,),
        in_specs=[pl.BlockSpec((tn, d), lambda i: (i, 0))] * 2,
        out_specs=pl.BlockSpec((tn, 2 * nw, LANES), lambda i: (i, 0, 0)),
        out_shape=jax.ShapeDtypeStruct((n, 2 * nw, LANES), jnp.uint32),
        compiler_params=_cparams(("parallel",)),
        name="peer_pack",
    )(u, v)


def peer_experts(idx, gate, xn, h, gain, uv, *, tb=64):
    M, D = xn.shape
    E = idx.shape[1]
    tb = min(tb, M)
    assert tb % PEER_SLOTS == 0 and M % tb == 0
    nb = M // tb
    idx3 = idx.reshape(nb, tb, E)
    ahead = jnp.concatenate([idx3[1:, :PEER_SLOTS], idx3[-1:, :PEER_SLOTS]], axis=0)
    idx_ext = jnp.concatenate([idx3, ahead], axis=1)
    spec = pl.BlockSpec((tb, D), lambda i: (i, 0))
    return pl.pallas_call(
        _peer_expert_kernel,
        grid=(nb,),
        in_specs=[
            pl.BlockSpec((1, tb + PEER_SLOTS, E), lambda i: (i, 0, 0), memory_space=pltpu.SMEM),
            pl.BlockSpec((tb, E), lambda i: (i, 0)),
            spec, spec,
            pl.BlockSpec((1, D), lambda i: (0, 0)),
            pl.BlockSpec(memory_space=pl.ANY),
        ],
        out_specs=spec,
        out_shape=jax.ShapeDtypeStruct((M, D), F32),
        scratch_shapes=[
            pltpu.VMEM((PEER_SLOTS // 2, D // LANES, E, LANES), jnp.uint32),
            pltpu.VMEM((PEER_SLOTS // 2, D // LANES, E, LANES), jnp.uint32),
            pltpu.VMEM((tb, D), F32),
            pltpu.SemaphoreType.DMA((PEER_SLOTS,)),
        ],
        compiler_params=_cparams(("arbitrary",)),
        name="peer_experts",
    )(idx_ext, gate, xn, h, gain.reshape(1, D), uv)


def _pack_w_in(w_in):
    D = w_in.shape[0]
    W = RWKV_WIDTH
    o = 0
    qk = w_in[:, o:o + 2 * MLSTM_QK]; o += 2 * MLSTM_QK
    v = w_in[:, o:o + MLSTM_V]; o += MLSTM_V
    og = w_in[:, o:o + MLSTM_V]; o += MLSTM_V
    ifg = w_in[:, o:o + 2 * MLSTM_HEADS]; o += 2 * MLSTM_HEADS
    rkv = w_in[:, o:o + 3 * W]; o += 3 * W
    wl = w_in[:, o:o + W_LORA]; o += W_LORA
    al = w_in[:, o:o + A_LORA]; o += A_LORA
    gl = w_in[:, o:o + G_LORA]; o += G_LORA
    gates = w_in[:, o:]

    def pad(t, n):
        return jnp.pad(t, ((0, 0), (0, n - t.shape[1])))

    w_mlstm = jnp.concatenate([qk, v, og, pad(ifg, LANES)], axis=1)
    w_rwkv = jnp.concatenate([rkv, pad(wl, LORA_PAD), pad(al, LORA_PAD), gl], axis=1)
    assert w_mlstm.shape == (D, MLSTM_PACKED) and w_rwkv.shape == (D, RWKV_PACKED)
    return w_mlstm.astype(BF16), w_rwkv.astype(BF16), gates.astype(BF16)


def kernel(x, norm_mix_gain, w_in, mlstm_conv, mlstm_b_i, mlstm_b_f, rwkv_mu, rwkv_w0, rwkv_w2,
           rwkv_a0, rwkv_a2, rwkv_g2, rwkv_k_k, rwkv_k_a, rwkv_r_k, rwkv_ln_w, rwkv_ln_b,
           proj_mlstm, proj_rwkv, w_out, norm_ffn_gain, peer_w_query, peer_sub_keys,
           peer_u, peer_v, norm_final_gain):
    B, T, D = x.shape
    assert w_in.shape[0] == 1, "the output norm is fused into the single layer's PEER kernel"
    l = 0
    x2d = x.reshape(B * T, D)
    w_mlstm, w_rwkv, w_gates = _pack_w_in(w_in[l])
    p_m = norm_matmul(x2d, norm_mix_gain[l], w_mlstm, tn=MLSTM_PACKED // 3).reshape(B, T, MLSTM_PACKED)
    p_r = norm_matmul(x2d, norm_mix_gain[l], w_rwkv, tn=RWKV_PACKED // 4).reshape(B, T, RWKV_PACKED)
    p_g = norm_matmul(x2d, norm_mix_gain[l], w_gates, tn=1024)

    y_m = mlstm_branch(p_m, mlstm_conv[l], mlstm_b_i[l], mlstm_b_f[l])
    r, w, k, v, aa, bb, g = rwkv_pre(p_r, rwkv_mu[l], rwkv_w0[l], rwkv_w2[l], rwkv_a0[l],
                                     rwkv_a2[l], rwkv_g2[l], rwkv_k_k[l], rwkv_k_a[l])
    y = rwkv_chunked(r, w, k, v, aa, bb)

    def flat(t):
        return t.reshape(B * T, RWKV_WIDTH)

    y_r = rwkv_post(flat(y), flat(r), flat(k), flat(v), flat(g),
                    rwkv_ln_w[l], rwkv_ln_b[l], rwkv_r_k[l])
    mixed = merge(y_m.reshape(B * T, MLSTM_V), y_r, p_g,
                  proj_mlstm[l].astype(BF16), proj_rwkv[l].astype(BF16))
    h2d, xn = out_proj(mixed, x2d, w_out[l].astype(BF16), norm_ffn_gain[l])
    q = matmul_split(xn, peer_w_query[l])
    idx, gate = peer_route(q, peer_sub_keys[l])
    out = peer_experts(idx, gate, xn, h2d, norm_final_gain, pack_expert_rows(peer_u[l], peer_v[l]))
    return out.reshape(B, T, D)
```

```python
import jax
import jax.numpy as jnp
from jax import lax
from jax.experimental import pallas as pl
from jax.experimental.pallas import tpu as pltpu

F32 = jnp.float32
BF16 = jnp.bfloat16

LANES = 128
SUBLANES = 8
VMEM_LIMIT_BYTES = 48 * 1024 * 1024

MLSTM_HEADS = 4
MLSTM_DK = 256
MLSTM_CONV = 4
MLSTM_CHUNK = 64
RWKV_HEADS = 16
RWKV_HEAD = 64
RWKV_WIDTH = RWKV_HEADS * RWKV_HEAD
W_LORA = 96
A_LORA = 96
G_LORA = 256
PEER_HEADS = 8
N_KEYS = 128
PEER_TOPK = 16
D_KEY = 256
RMS_EPS = 1e-6
GN_EPS = 64e-5
L2_EPS = 1e-12

PEER_CAND = PEER_TOPK + (SUBLANES - 1) * SUBLANES + (PEER_TOPK - SUBLANES)
PEER_GROUP = 8
PEER_RING = 4
PEER_AHEAD = 2
PEER_SLOTS = PEER_GROUP * PEER_RING
PEER_UNROLL = 1

MLSTM_QK = MLSTM_HEADS * MLSTM_DK
MLSTM_V = MLSTM_QK
LORA_PAD = 128

COL_QK = 0
COL_V = COL_QK + 2 * MLSTM_QK
COL_O = COL_V + MLSTM_V
COL_IF = COL_O + MLSTM_V
MLSTM_PACKED = COL_IF + LANES
RWKV_PACKED = 3 * RWKV_WIDTH + 2 * LORA_PAD + G_LORA


def _cparams(sem):
    return pltpu.CompilerParams(dimension_semantics=sem, vmem_limit_bytes=VMEM_LIMIT_BYTES)


def _split2(x):
    hi = x.astype(BF16)
    lo = (x - hi.astype(F32)).astype(BF16)
    return hi, lo


def _split3(x):
    hi = x.astype(BF16)
    r1 = x - hi.astype(F32)
    mid = r1.astype(BF16)
    lo = (r1 - mid.astype(F32)).astype(BF16)
    return hi, mid, lo


def _dot(a, b):
    return jnp.dot(a, b, preferred_element_type=F32)


def _dot_exact_rhs(x, ones_bf16):
    hi, mid, lo = _split3(x)
    return _dot(hi, ones_bf16) + _dot(mid, ones_bf16) + _dot(lo, ones_bf16)


def _norm_matmul_kernel(x_ref, g_ref, w_ref, o_ref, xn_ref):
    @pl.when(pl.program_id(1) == 0)
    def _():
        x = x_ref[...]
        ms = jnp.mean(x * x, axis=-1, keepdims=True)
        xn_ref[...] = (x * lax.rsqrt(ms + RMS_EPS) * g_ref[...]).astype(BF16)

    o_ref[...] = _dot(xn_ref[...], w_ref[...])


def norm_matmul(x, gain, w_bf16, *, tm=1024, tn=384):
    M, K = x.shape
    N = w_bf16.shape[1]
    tm = min(tm, M)
    return pl.pallas_call(
        _norm_matmul_kernel,
        grid=(M // tm, N // tn),
        in_specs=[
            pl.BlockSpec((tm, K), lambda i, j: (i, 0)),
            pl.BlockSpec((1, K), lambda i, j: (0, 0)),
            pl.BlockSpec((K, tn), lambda i, j: (0, j)),
        ],
        out_specs=pl.BlockSpec((tm, tn), lambda i, j: (i, j)),
        out_shape=jax.ShapeDtypeStruct((M, N), F32),
        scratch_shapes=[pltpu.VMEM((tm, K), BF16)],
        compiler_params=_cparams(("parallel", "arbitrary")),
        name="norm_matmul",
    )(x, gain.reshape(1, K), w_bf16)


def _mlstm_kernel(qk_ref, v_ref, o_ref, if_ref, conv_ref, bias_ref, y_ref,
                  ext_ref, c_ref, n_ref, m_ref):
    L = MLSTM_CHUNK
    DK = MLSTM_DK
    step = pl.program_id(1)

    @pl.when(step == 0)
    def _():
        ext_ref[0:SUBLANES, :] = jnp.zeros((SUBLANES, 2 * MLSTM_QK), F32)
        c_ref[...] = jnp.zeros_like(c_ref)
        n_ref[...] = jnp.zeros_like(n_ref)
        m_ref[...] = jnp.zeros_like(m_ref)

    ext_ref[SUBLANES:SUBLANES + L, :] = qk_ref[0]
    acc = jnp.zeros((L, 2 * MLSTM_QK), F32)
    for j in range(MLSTM_CONV):
        off = SUBLANES - (MLSTM_CONV - 1) + j
        acc = acc + ext_ref[off:off + L, :] * conv_ref[j:j + 1, :]
    ext_ref[0:SUBLANES, :] = qk_ref[0, L - SUBLANES:L, :]
    qk = acc * jax.nn.sigmoid(acc)

    row = lax.broadcasted_iota(jnp.int32, (L, L), 0)
    col = lax.broadcasted_iota(jnp.int32, (L, L), 1)
    causal = col <= row
    eye = col == row

    def to_row(x_col):
        return jnp.sum(jnp.where(eye, x_col, 0.0), axis=0, keepdims=True)

    gates = if_ref[0]
    for h in range(MLSTM_HEADS):
        q = qk[:, h * DK:(h + 1) * DK] * (DK ** -0.5)
        k = qk[:, MLSTM_QK + h * DK:MLSTM_QK + (h + 1) * DK]
        v = v_ref[0, :, h * DK:(h + 1) * DK]
        ig_col = gates[:, h:h + 1] + bias_ref[0:1, h:h + 1]
        lf_col = jax.nn.log_sigmoid(
            gates[:, MLSTM_HEADS + h:MLSTM_HEADS + h + 1]
            + bias_ref[1:2, h:h + 1])
        lf_row = to_row(lf_col)
        ig_row = to_row(ig_col)
        b_col = jnp.sum(jnp.where(causal, lf_row, 0.0), axis=1, keepdims=True)
        b_row = to_row(b_col)
        b_last = b_col[L - 1:L, :]
        m_prev = m_ref[h:h + 1, 0:1]
        C = c_ref[h]
        n_row = n_ref[h:h + 1, :]

        dmat = jnp.where(causal, b_col - b_row + ig_row, -jnp.inf)
        inter = b_col + m_prev
        m_t = jnp.maximum(inter, jnp.max(dmat, axis=1, keepdims=True))
        qb = q.astype(BF16)
        kb = k.astype(BF16)
        vb = v.astype(BF16)
        s = lax.dot_general(qb, kb, (((1,), (1,)), ((), ())),
                            preferred_element_type=F32) * jnp.exp(dmat - m_t)
        w_inter = jnp.exp(inter - m_t)
        num = _dot(s.astype(BF16), vb) + w_inter * _dot(qb, C.astype(BF16))
        qn = jnp.sum(q * n_row, axis=1, keepdims=True)
        den = jnp.sum(s, axis=1, keepdims=True) + w_inter * qn
        hh = num / jnp.maximum(jnp.abs(den), jnp.exp(-m_t))
        o = o_ref[0, :, h * DK:(h + 1) * DK]
        y_ref[0, :, h * DK:(h + 1) * DK] = (jax.nn.sigmoid(o) * hh).astype(y_ref.dtype)

        g_end = b_last - b_col + ig_col
        m_new = jnp.maximum(b_last + m_prev, jnp.max(g_end, axis=0, keepdims=True))
        decay = jnp.exp(b_last + m_prev - m_new)
        ws = jnp.exp(g_end - m_new)
        kw = k * ws
        c_ref[h] = decay * C + lax.dot_general(
            kw.astype(BF16), vb, (((0,), (0,)), ((), ())), preferred_element_type=F32)
        n_ref[h:h + 1, :] = decay * n_row + jnp.sum(kw, axis=0, keepdims=True)
        m_ref[h:h + 1, :] = jnp.broadcast_to(m_new, (1, LANES))


def mlstm_branch(p, conv_w, b_i, b_f):
    B, T, _ = p.shape
    L = MLSTM_CHUNK
    bias = jnp.zeros((SUBLANES, LANES), F32)
    bias = bias.at[0, :MLSTM_HEADS].set(b_i).at[1, :MLSTM_HEADS].set(b_f)
    nqk = 2 * MLSTM_QK
    return pl.pallas_call(
        _mlstm_kernel,
        grid=(B, T // L),
        in_specs=[
            pl.BlockSpec((1, L, nqk), lambda b, c: (b, c, COL_QK // nqk)),
            pl.BlockSpec((1, L, MLSTM_V), lambda b, c: (b, c, COL_V // MLSTM_V)),
            pl.BlockSpec((1, L, MLSTM_V), lambda b, c: (b, c, COL_O // MLSTM_V)),
            pl.BlockSpec((1, L, LANES), lambda b, c: (b, c, COL_IF // LANES)),
            pl.BlockSpec((MLSTM_CONV, nqk), lambda b, c: (0, 0)),
            pl.BlockSpec((SUBLANES, LANES), lambda b, c: (0, 0)),
        ],
        out_specs=pl.BlockSpec((1, L, MLSTM_V), lambda b, c: (b, c, 0)),
        out_shape=jax.ShapeDtypeStruct((B, T, MLSTM_V), BF16),
        scratch_shapes=[
            pltpu.VMEM((SUBLANES + L, nqk), F32),
            pltpu.VMEM((MLSTM_HEADS, MLSTM_DK, MLSTM_DK), F32),
            pltpu.VMEM((SUBLANES, MLSTM_DK), F32),
            pltpu.VMEM((SUBLANES, LANES), F32),
        ],
        compiler_params=_cparams(("parallel", "arbitrary")),
        name="mlstm",
    )(p, p, p, p, conv_w, bias)


def _head_sum_matrix(group):
    r = lax.broadcasted_iota(jnp.int32, (LANES, LANES), 0) // group
    c = lax.broadcasted_iota(jnp.int32, (LANES, LANES), 1) // group
    return jnp.where(r == c, 1.0, 0.0).astype(BF16)


def _rwkv_pre_kernel(p_ref, prev_ref, mu_ref, vec_ref, w2_ref, a2_ref, g2_ref,
                     r_ref, w_ref, k_ref, v_ref, aa_ref, bb_ref, g_ref):
    W = RWKV_WIDTH
    tb = p_ref.shape[1]
    p = p_ref[0]
    last = prev_ref[0, SUBLANES - 1:SUBLANES, :]
    last = jnp.where(pl.program_id(1) == 0, 0.0, last)
    rid = lax.broadcasted_iota(jnp.int32, (tb, 1), 0)
    shifted = jnp.where(rid == 0, last, pltpu.roll(p, 1, 0))
    p = p + (shifted - p) * mu_ref[...]

    r = p[:, 0:W]
    k = p[:, W:2 * W]
    v = p[:, 2 * W:3 * W]
    wl = p[:, 3 * W:3 * W + LORA_PAD]
    al = p[:, 3 * W + LORA_PAD:3 * W + 2 * LORA_PAD]
    gl = p[:, 3 * W + 2 * LORA_PAD:]
    w0 = vec_ref[0:1, :]
    a0 = vec_ref[1:2, :]
    k_k = vec_ref[2:3, :]
    k_a = vec_ref[3:4, :]

    wx = -(w0 + _dot(jnp.tanh(wl).astype(BF16), w2_ref[...]))
    softplus = jnp.maximum(wx, 0.0) + jnp.log1p(jnp.exp(-jnp.abs(wx)))
    w = -softplus - 0.5
    log_decay = -jnp.exp(w)
    a = jax.nn.sigmoid(a0 + _dot(al.astype(BF16), a2_ref[...]))
    g = _dot(jax.nn.sigmoid(gl).astype(BF16), g2_ref[...])

    kk = k * k_k
    ones = _head_sum_matrix(RWKV_HEAD)
    sq = kk * kk
    ss = jnp.concatenate(
        [_dot_exact_rhs(sq[:, j * LANES:(j + 1) * LANES], ones) for j in range(W // LANES)],
        axis=1)
    kk = kk / jnp.maximum(jnp.sqrt(ss), L2_EPS)

    r_ref[0] = r
    w_ref[0] = log_decay
    k_ref[0] = k * (1.0 + (a - 1.0) * k_a)
    v_ref[0] = v
    aa_ref[0] = -kk
    bb_ref[0] = kk * a
    g_ref[0] = g


def rwkv_pre(p, mu, w0, w2, a0, a2, g2, k_k, k_a, *, tb=256):
    B, T, _ = p.shape
    W = RWKV_WIDTH
    tb = min(tb, T)
    mu_p = jnp.zeros((1, RWKV_PACKED), F32)
    mu_p = mu_p.at[0, :3 * W].set(mu[:3 * W])
    mu_p = mu_p.at[0, 3 * W:3 * W + W_LORA].set(mu[3 * W:3 * W + W_LORA])
    mu_p = mu_p.at[0, 3 * W + LORA_PAD:3 * W + LORA_PAD + A_LORA].set(
        mu[3 * W + W_LORA:3 * W + W_LORA + A_LORA])
    mu_p = mu_p.at[0, 3 * W + 2 * LORA_PAD:].set(mu[3 * W + W_LORA + A_LORA:])
    vec = jnp.zeros((SUBLANES, W), F32)
    vec = vec.at[0].set(w0).at[1].set(a0).at[2].set(k_k).at[3].set(k_a)
    w2_p = jnp.zeros((LORA_PAD, W), F32).at[:W_LORA].set(w2).astype(BF16)
    a2_p = jnp.zeros((LORA_PAD, W), F32).at[:A_LORA].set(a2).astype(BF16)
    nprev = tb // SUBLANES
    out = jax.ShapeDtypeStruct((B, T, W), F32)
    ospec = pl.BlockSpec((1, tb, W), lambda b, i: (b, i, 0))
    return pl.pallas_call(
        _rwkv_pre_kernel,
        grid=(B, T // tb),
        in_specs=[
            pl.BlockSpec((1, tb, RWKV_PACKED), lambda b, i: (b, i, 0)),
            pl.BlockSpec((1, SUBLANES, RWKV_PACKED),
                         lambda b, i: (b, jnp.maximum(i * nprev - 1, 0), 0)),
            pl.BlockSpec((1, RWKV_PACKED), lambda b, i: (0, 0)),
            pl.BlockSpec((SUBLANES, W), lambda b, i: (0, 0)),
            pl.BlockSpec((LORA_PAD, W), lambda b, i: (0, 0)),
            pl.BlockSpec((LORA_PAD, W), lambda b, i: (0, 0)),
            pl.BlockSpec((G_LORA, W), lambda b, i: (0, 0)),
        ],
        out_specs=[ospec] * 7,
        out_shape=[out] * 7,
        compiler_params=_cparams(("parallel", "parallel")),
        name="rwkv_pre",
    )(p, p, mu_p, vec, w2_p, a2_p, g2.astype(BF16))


RWKV_CHUNK = 64


def _mm3(a, b, dims):
    ah, al = _split2(a)
    bh, bl = _split2(b)

    def dg(x, y):
        return lax.dot_general(x, y, (dims, ((), ())), preferred_element_type=F32)

    return dg(ah, bh) + dg(al, bh) + dg(ah, bl)


_NN = ((1,), (0,))
_NT = ((1,), (1,))
_TN = ((0,), (0,))


def _rwkv_chunk_kernel(r_ref, lw_ref, k_ref, v_ref, aa_ref, bb_ref, q_ref, y0_ref, p_ref, g_ref):
    L = RWKV_CHUNK
    N = RWKV_HEAD
    W = RWKV_WIDTH
    npair = W // LANES
    row = lax.broadcasted_iota(jnp.int32, (L, LANES), 0)
    lane = lax.broadcasted_iota(jnp.int32, (L, LANES), 1)
    first = lane < N
    pos = lane % N
    strict = pos < row
    lower = pos <= row
    eye = pos == row

    def bd(x):
        return jnp.concatenate([jnp.where(first, x, 0.0), jnp.where(first, 0.0, x)], axis=0)

    def unbd(x):
        return jnp.where(first, x[0:L], x[L:2 * L])

    tri = jnp.where(lax.broadcasted_iota(jnp.int32, (L, L), 1)
                    <= lax.broadcasted_iota(jnp.int32, (L, L), 0), 1.0, 0.0).astype(BF16)
    lw = lw_ref[0]
    c = _dot_exact_rhs_left(tri, lw)
    c_last = c[L - 1:L, :]
    e_pos = jnp.exp(c)
    e_neg = jnp.exp(-c)
    e_hat = jnp.exp(c_last - c)
    a_all = aa_ref[0]
    b_all = bb_ref[0]
    k_all = k_ref[0]
    at_all = a_all * jnp.exp(c - lw)
    rt_all = r_ref[0] * e_pos
    bt_all = b_all * e_neg
    kt_all = k_all * e_neg
    bh_all = b_all * e_hat
    kh_all = k_all * e_hat
    gl_all = jnp.exp(c_last)

    def pair(x, q):
        return x[:, q * LANES:(q + 1) * LANES]

    pairs = range(npair)
    at = [pair(at_all, q) for q in pairs]
    rt = [pair(rt_all, q) for q in pairs]
    v = [pair(v_ref[0], q) for q in pairs]
    bt_bd = [bd(pair(bt_all, q)) for q in pairs]
    kt_bd = [bd(pair(kt_all, q)) for q in pairs]
    v_bd = [bd(x) for x in v]

    ar = [jnp.concatenate([at[q], rt[q]], axis=0) for q in pairs]
    sb = [_mm3(ar[q], bt_bd[q], _NT) for q in pairs]
    sk = [_mm3(ar[q], kt_bd[q], _NT) for q in pairs]
    aab = [jnp.where(strict, sb[q][0:L], 0.0) for q in pairs]
    mrb = [jnp.where(lower, sb[q][L:2 * L], 0.0) for q in pairs]
    aak_mrk = [jnp.concatenate([jnp.where(strict, sk[q][0:L], 0.0),
                                jnp.where(lower, sk[q][L:2 * L], 0.0)], axis=0) for q in pairs]

    t = [jnp.where(eye, 1.0, 0.0) + aab[q] for q in pairs]
    x = [_mm3(aab[q], bd(aab[q]), _NN) for q in pairs]
    n = 2
    while 2 * n < L:
        tx = [_mm3(jnp.concatenate([t[q], x[q]], axis=0), bd(x[q]), _NN) for q in pairs]
        t = [t[q] + tx[q][0:L] for q in pairs]
        x = [tx[q][L:2 * L] for q in pairs]
        n *= 2
    t = [t[q] + _mm3(t[q], bd(x[q]), _NN) for q in pairs]

    av_mv = [_mm3(aak_mrk[q], v_bd[q], _NN) for q in pairs]
    wu = [_mm3(t[q], jnp.concatenate([bd(at[q]), bd(av_mv[q][0:L])], axis=1), _NN)
          for q in pairs]
    w = [wu[q][:, 0:LANES] for q in pairs]
    u0 = [wu[q][:, LANES:2 * LANES] for q in pairs]
    qy = [_mm3(mrb[q], jnp.concatenate([bd(w[q]), bd(u0[q])], axis=1), _NN) for q in pairs]
    qq = [rt[q] + qy[q][:, 0:LANES] for q in pairs]
    y0 = [qy[q][:, LANES:2 * LANES] + av_mv[q][L:2 * L] for q in pairs]
    bh = [pair(bh_all, q) for q in pairs]
    kh = [pair(kh_all, q) for q in pairs]
    pg = [_mm3(bh[q], wu[q], _TN) for q in pairs]
    kv = [_mm3(kh[q], v[q], _TN) for q in pairs]
    pp = [unbd(pg[q][:, 0:LANES]) + jnp.where(eye, pair(gl_all, q), 0.0) for q in pairs]
    gg = [unbd(pg[q][:, LANES:2 * LANES]) + unbd(kv[q]) for q in pairs]
    for q in pairs:
        sl = slice(q * LANES, (q + 1) * LANES)
        q_ref[0, :, sl] = qq[q]
        y0_ref[0, :, sl] = y0[q]
        p_ref[0, :, sl] = pp[q]
        g_ref[0, :, sl] = gg[q]


def _dot_exact_rhs_left(ones_bf16, x):
    hi, mid, lo = _split3(x)
    return _dot(ones_bf16, hi) + _dot(ones_bf16, mid) + _dot(ones_bf16, lo)


def _rwkv_state_kernel(q_ref, y0_ref, p_ref, g_ref, y_ref, h_ref):
    L = RWKV_CHUNK
    N = RWKV_HEAD
    npair = RWKV_WIDTH // LANES

    @pl.when(pl.program_id(1) == 0)
    def _():
        h_ref[...] = jnp.zeros_like(h_ref)

    first = lax.broadcasted_iota(jnp.int32, (N, LANES), 1) < N

    def bd(x):
        return jnp.concatenate([jnp.where(first, x, 0.0), jnp.where(first, 0.0, x)], axis=0)

    pairs = range(npair)
    sls = [slice(q * LANES, (q + 1) * LANES) for q in pairs]
    h_bd = [bd(h_ref[:, sls[q]]) for q in pairs]
    y = [_mm3(q_ref[0, :, sls[q]], h_bd[q], _NN) for q in pairs]
    hn = [_mm3(p_ref[0, :, sls[q]], h_bd[q], _NN) for q in pairs]
    for q in pairs:
        y_ref[0, :, sls[q]] = y[q] + y0_ref[0, :, sls[q]]
        h_ref[:, sls[q]] = hn[q] + g_ref[0, :, sls[q]]


def rwkv_chunked(r, lw, k, v, aa, bb):
    B, T, W = r.shape
    L = RWKV_CHUNK
    spec = pl.BlockSpec((1, L, W), lambda b, c: (b, c, 0))
    out = jax.ShapeDtypeStruct((B, T, W), F32)
    q, y0, p, g = pl.pallas_call(
        _rwkv_chunk_kernel,
        grid=(B, T // L),
        in_specs=[spec] * 6,
        out_specs=[spec] * 4,
        out_shape=[out] * 4,
        compiler_params=_cparams(("parallel", "parallel")),
        name="rwkv_chunk",
    )(r, lw, k, v, aa, bb)
    return pl.pallas_call(
        _rwkv_state_kernel,
        grid=(B, T // L),
        in_specs=[spec] * 4,
        out_specs=spec,
        out_shape=out,
        scratch_shapes=[pltpu.VMEM((RWKV_HEAD, W), F32)],
        compiler_params=_cparams(("parallel", "arbitrary")),
        name="rwkv_state",
    )(q, y0, p, g)


def _rwkv_post_kernel(y_ref, r_ref, k_ref, v_ref, g_ref, vec_ref, o_ref):
    W = RWKV_WIDTH
    ones = _head_sum_matrix(RWKV_HEAD)
    ln_w = vec_ref[0:1, :]
    ln_b = vec_ref[1:2, :]
    r_k = vec_ref[2:3, :]

    def head_sum(x):
        return jnp.concatenate(
            [_dot_exact_rhs(x[:, j * LANES:(j + 1) * LANES], ones) for j in range(W // LANES)],
            axis=1)

    y = y_ref[...]
    mean = head_sum(y) * (1.0 / RWKV_HEAD)
    d = y - mean
    var = head_sum(d * d) * (1.0 / RWKV_HEAD)
    yn = d * lax.rsqrt(var + GN_EPS) * ln_w + ln_b
    bonus = head_sum(r_ref[...] * k_ref[...] * r_k) * v_ref[...]
    o_ref[...] = ((yn + bonus) * g_ref[...]).astype(o_ref.dtype)


def rwkv_post(y, r, k, v, g, ln_w, ln_b, r_k, *, tb=256):
    M, W = y.shape
    tb = min(tb, M)
    vec = jnp.zeros((SUBLANES, W), F32)
    vec = vec.at[0].set(ln_w).at[1].set(ln_b).at[2].set(r_k.reshape(W))
    spec = pl.BlockSpec((tb, W), lambda i: (i, 0))
    return pl.pallas_call(
        _rwkv_post_kernel,
        grid=(M // tb,),
        in_specs=[spec] * 5 + [pl.BlockSpec((SUBLANES, W), lambda i: (0, 0))],
        out_specs=spec,
        out_shape=jax.ShapeDtypeStruct((M, W), BF16),
        compiler_params=_cparams(("parallel",)),
        name="rwkv_post",
    )(y, r, k, v, g, vec)


def _merge_kernel(ym_ref, yr_ref, gm_ref, gr_ref, pm_ref, pr_ref, o_ref):
    m = _dot(ym_ref[...], pm_ref[...])
    r = _dot(yr_ref[...], pr_ref[...])
    o_ref[...] = (jax.nn.sigmoid(gm_ref[...]) * m
                  + jax.nn.sigmoid(gr_ref[...]) * r).astype(o_ref.dtype)


def merge(y_m, y_r, gates, proj_m, proj_r, *, tm=512, tn=2048):
    M, K = y_m.shape
    D = proj_m.shape[1]
    tm = min(tm, M)
    return pl.pallas_call(
        _merge_kernel,
        grid=(M // tm, D // tn),
        in_specs=[
            pl.BlockSpec((tm, K), lambda i, j: (i, 0)),
            pl.BlockSpec((tm, K), lambda i, j: (i, 0)),
            pl.BlockSpec((tm, tn), lambda i, j: (i, j)),
            pl.BlockSpec((tm, tn), lambda i, j: (i, D // tn + j)),
            pl.BlockSpec((K, tn), lambda i, j: (0, j)),
            pl.BlockSpec((K, tn), lambda i, j: (0, j)),
        ],
        out_specs=pl.BlockSpec((tm, tn), lambda i, j: (i, j)),
        out_shape=jax.ShapeDtypeStruct((M, D), BF16),
        compiler_params=_cparams(("parallel", "arbitrary")),
        name="merge",
    )(y_m, y_r, gates, gates, proj_m, proj_r)


def _out_proj_kernel(mix_ref, x_ref, w_ref, g_ref, h_ref, xn_ref):
    h = x_ref[...] + _dot(mix_ref[...], w_ref[...])
    h_ref[...] = h
    ms = jnp.mean(h * h, axis=-1, keepdims=True)
    xn_ref[...] = h * lax.rsqrt(ms + RMS_EPS) * g_ref[...]


def out_proj(mixed, x2d, w_out_bf16, gain, *, tm=256):
    M, D = x2d.shape
    tm = min(tm, M)
    spec = pl.BlockSpec((tm, D), lambda i: (i, 0))
    return pl.pallas_call(
        _out_proj_kernel,
        grid=(M // tm,),
        in_specs=[spec, spec, pl.BlockSpec((D, D), lambda i: (0, 0)),
                  pl.BlockSpec((1, D), lambda i: (0, 0))],
        out_specs=[spec, spec],
        out_shape=[jax.ShapeDtypeStruct((M, D), F32)] * 2,
        compiler_params=_cparams(("parallel",)),
        name="out_proj",
    )(mixed, x2d, w_out_bf16, gain.reshape(1, D))


def _matmul_split_kernel(a_ref, whi_ref, wlo_ref, o_ref, hi_ref, lo_ref):
    @pl.when(pl.program_id(1) == 0)
    def _():
        hi, lo = _split2(a_ref[...])
        hi_ref[...] = hi
        lo_ref[...] = lo

    o_ref[...] = (_dot(hi_ref[...], whi_ref[...]) + _dot(lo_ref[...], whi_ref[...])
                  + _dot(hi_ref[...], wlo_ref[...]))


def matmul_split(a, w, *, tm=512, tn=512):
    M, K = a.shape
    N = w.shape[1]
    tm = min(tm, M)
    whi = w.astype(BF16)
    wlo = (w - whi.astype(F32)).astype(BF16)
    return pl.pallas_call(
        _matmul_split_kernel,
        grid=(M // tm, N // tn),
        in_specs=[pl.BlockSpec((tm, K), lambda i, j: (i, 0)),
                  pl.BlockSpec((K, tn), lambda i, j: (0, j)),
                  pl.BlockSpec((K, tn), lambda i, j: (0, j))],
        out_specs=pl.BlockSpec((tm, tn), lambda i, j: (i, j)),
        out_shape=jax.ShapeDtypeStruct((M, N), F32),
        scratch_shapes=[pltpu.VMEM((tm, K), BF16), pltpu.VMEM((tm, K), BF16)],
        compiler_params=_cparams(("parallel", "arbitrary")),
        name="peer_query",
    )(a, whi, wlo)


def _route_kernel(q_ref, khi_ref, klo_ref, idx_ref, gate_ref,
                  s_ref, tv_ref, tp_ref, c_ref, e_ref, bv_ref, bi_ref):
    tn = q_ref.shape[0]
    KK = PEER_TOPK
    half = D_KEY // 2
    NEG = -jnp.inf
    key_id = lax.broadcasted_iota(jnp.int32, (N_KEYS, tn), 0).astype(F32)
    crow = lax.broadcasted_iota(jnp.int32, (PEER_CAND, tn), 0)
    mid = crow - KK
    cand_id = jnp.where(
        crow < KK, crow,
        jnp.where(crow < PEER_CAND - SUBLANES,
                  (mid // SUBLANES + 1) * KK + mid % SUBLANES,
                  (crow - (PEER_CAND - SUBLANES) + SUBLANES) * KK)).astype(F32)
    rank = lax.broadcasted_iota(jnp.int32, (KK, tn), 0)

    def nt(a, b):
        return lax.dot_general(a, b, (((1,), (1,)), ((), ())), preferred_element_type=F32)

    for hp in range(2 * PEER_HEADS):
        qhi, qlo = _split2(q_ref[:, hp * half:(hp + 1) * half])
        p = hp % 2
        s_ref[hp] = nt(khi_ref[p], qhi) + nt(khi_ref[p], qlo) + nt(klo_ref[p], qhi)
    tv_ref[...] = jnp.zeros_like(tv_ref)
    tp_ref[...] = jnp.zeros_like(tp_ref)
    bv_ref[...] = jnp.zeros_like(bv_ref)
    bi_ref[...] = jnp.zeros_like(bi_ref)

    def sub_topk(kk, carry):
        for hp in range(2 * PEER_HEADS):
            s = s_ref[hp]
            m = jnp.max(s, axis=0, keepdims=True)
            pos = jnp.min(jnp.where(s == m, key_id, float(N_KEYS)), axis=0, keepdims=True)
            s_ref[hp] = jnp.where(key_id == pos, NEG, s)
            tv_ref[hp] = jnp.where(rank == kk, m, tv_ref[hp])
            tp_ref[hp] = jnp.where(rank == kk, pos, tp_ref[hp])
        return carry

    lax.fori_loop(0, KK, sub_topk, 0)

    grp = lax.broadcasted_iota(jnp.int32, (SUBLANES, tn), 0)
    for h in range(PEER_HEADS):
        v1, v2 = tv_ref[2 * h], tv_ref[2 * h + 1]
        p1, p2 = tp_ref[2 * h], tp_ref[2 * h + 1]
        c_ref[h, 0:KK, :] = v1[0:1, :] + v2
        e_ref[h, 0:KK, :] = p1[0:1, :] * float(N_KEYS) + p2
        for i in range(1, SUBLANES):
            keep = grp < KK // (i + 1)
            rows = slice(KK + (i - 1) * SUBLANES, KK + i * SUBLANES)
            c_ref[h, rows, :] = jnp.where(keep, v1[i:i + 1, :] + v2[0:SUBLANES, :], NEG)
            e_ref[h, rows, :] = jnp.where(
                keep, p1[i:i + 1, :] * float(N_KEYS) + p2[0:SUBLANES, :], -1.0)
        c_ref[h, PEER_CAND - SUBLANES:PEER_CAND, :] = v1[SUBLANES:KK, :] + v2[0:1, :]
        e_ref[h, PEER_CAND - SUBLANES:PEER_CAND, :] = p1[SUBLANES:KK, :] * float(N_KEYS) + p2[0:1, :]

    def cand_topk(kk, carry):
        for h in range(PEER_HEADS):
            c = c_ref[h]
            m = jnp.max(c, axis=0, keepdims=True)
            pos = jnp.min(jnp.where(c == m, cand_id, float(KK * KK)), axis=0, keepdims=True)
            hit = cand_id == pos
            eid = jnp.max(jnp.where(hit, e_ref[h], -1.0), axis=0, keepdims=True)
            c_ref[h] = jnp.where(hit, NEG, c)
            rows = slice(h * KK, (h + 1) * KK)
            bv_ref[rows, :] = jnp.where(rank == kk, m, bv_ref[rows, :])
            bi_ref[rows, :] = jnp.where(rank == kk, eid, bi_ref[rows, :])
        return carry

    lax.fori_loop(0, KK, cand_topk, 0)

    for h in range(PEER_HEADS):
        rows = slice(h * KK, (h + 1) * KK)
        b = bv_ref[rows, :]
        e = jnp.exp(b - b[0:1, :])
        bv_ref[rows, :] = e / jnp.sum(e, axis=0, keepdims=True)
    gate_ref[...] = bv_ref[...].T
    idx_ref[...] = bi_ref[...].T.astype(jnp.int32)


def peer_route(q, sub_keys, *, tn=128):
    M = q.shape[0]
    tn = min(tn, M)
    khi = sub_keys.astype(BF16)
    klo = (sub_keys - khi.astype(F32)).astype(BF16)
    kspec = pl.BlockSpec((2, N_KEYS, D_KEY // 2), lambda i: (0, 0, 0))
    ospec = pl.BlockSpec((tn, N_KEYS), lambda i: (i, 0))
    return pl.pallas_call(
        _route_kernel,
        grid=(M // tn,),
        in_specs=[pl.BlockSpec((tn, PEER_HEADS * D_KEY), lambda i: (i, 0)), kspec, kspec],
        out_specs=[ospec, ospec],
        out_shape=[jax.ShapeDtypeStruct((M, N_KEYS), jnp.int32),
                   jax.ShapeDtypeStruct((M, N_KEYS), F32)],
        scratch_shapes=[
            pltpu.VMEM((2 * PEER_HEADS, N_KEYS, tn), F32),
            pltpu.VMEM((2 * PEER_HEADS, PEER_TOPK, tn), F32),
            pltpu.VMEM((2 * PEER_HEADS, PEER_TOPK, tn), F32),
            pltpu.VMEM((PEER_HEADS, PEER_CAND, tn), F32),
            pltpu.VMEM((PEER_HEADS, PEER_CAND, tn), F32),
            pltpu.VMEM((PEER_HEADS * PEER_TOPK, tn), F32),
            pltpu.VMEM((PEER_HEADS * PEER_TOPK, tn), F32),
        ],
        compiler_params=_cparams(("parallel",)),
        name="peer_route",
    )(q, khi, klo)


def _gelu_exact(x):
    return 0.5 * x * (1.0 + lax.erf(x * (2.0 ** -0.5)))


def _peer_expert_kernel(idx_ref, gate_ref, xn_ref, h_ref, gain_ref, uv_ref, o_ref,
                        buf0_ref, buf1_ref, buf2_ref, buf3_ref, x3_ref, acc3_ref, acc_ref, sem):
    tb, D = xn_ref.shape
    E = idx_ref.shape[2]
    nt = D // LANES
    nw = nt // 2
    step = pl.program_id(0)
    bufs = (buf0_ref, buf1_ref, buf2_ref, buf3_ref)
    span = PEER_GROUP * PEER_AHEAD

    def staging(slot):
        return bufs[slot // PEER_GROUP].at[slot % PEER_GROUP]

    def slot_copy(slot):
        return pltpu.make_async_copy(staging(slot), staging(slot), sem.at[slot])

    def as_bf16_rows(words):
        return pltpu.bitcast(words.reshape(E * nw, LANES), BF16)

    def issue(t, slot):
        for e in range(E):
            pltpu.make_async_copy(uv_ref.at[idx_ref[0, t, e]], staging(slot).at[:, e],
                                  sem.at[slot]).start(priority=e % 2)

    @pl.when(step == 0)
    def _():
        for t in range(span):
            issue(t, t)

    for s in range(nt):
        x3_ref[:, s, :] = xn_ref[:, s * LANES:(s + 1) * LANES]

    ncol = E * nt
    col = lax.broadcasted_iota(jnp.int32, (nt, ncol), 1)
    same_tile = jnp.where(lax.broadcasted_iota(jnp.int32, (nt, ncol), 0) == col % nt, 1.0, 0.0)
    spread = jnp.where(lax.broadcasted_iota(jnp.int32, (E, ncol), 0)
                       == lax.broadcasted_iota(jnp.int32, (E, ncol), 1) // nt, 1.0, 0.0).astype(BF16)
    gather = jnp.where(lax.broadcasted_iota(jnp.int32, (ncol, E), 0) // nt
                       == lax.broadcasted_iota(jnp.int32, (ncol, E), 1), 1.0, 0.0).astype(BF16)

    def body(it, carry):
        for rnd in range(PEER_UNROLL):
            ring_round(pl.multiple_of((it * PEER_UNROLL + rnd) * PEER_SLOTS, PEER_SLOTS))
        return carry

    def ring_round(base):
        gates = gate_ref[pl.ds(base, PEER_SLOTS), :]
        for g in range(0, PEER_SLOTS, PEER_GROUP):
            grp = range(g, g + PEER_GROUP)
            for j in grp:
                issue(base + j + span, (j + span) % PEER_SLOTS)
            for j in grp:
                slot_copy(j).wait()
            u2 = [as_bf16_rows(staging(j)[0]) for j in grp]
            xs = [_split2(x3_ref[base + j]) for j in grp]
            s2 = [lax.dot_general(jnp.concatenate(x, axis=0), u, (_NT, ((), ())),
                                  preferred_element_type=F32) for x, u in zip(xs, u2)]
            picked = jnp.concatenate(
                [jnp.sum((s[0:nt] + s[nt:2 * nt]) * same_tile, axis=0, keepdims=True) for s in s2],
                axis=0)
            act = _dot_exact_rhs(picked, gather)
            c = gates[g:g + PEER_GROUP] * _gelu_exact(act)
            c_hi, c_lo = _split2(c)
            cx = _dot(jnp.concatenate([c_hi, c_lo], axis=0), spread)
            c2 = [jnp.concatenate([cx[k:k + 1] * same_tile,
                                   cx[PEER_GROUP + k:PEER_GROUP + k + 1] * same_tile],
                                  axis=0).astype(BF16) for k in range(PEER_GROUP)]
            o2 = [_dot(cc, as_bf16_rows(staging(j)[1])) for j, cc in zip(grp, c2)]
            for j, o in zip(grp, o2):
                acc3_ref[base + j] = o[0:nt] + o[nt:2 * nt]

    lax.fori_loop(0, tb // (PEER_SLOTS * PEER_UNROLL), body, 0)

    @pl.when(step == pl.num_programs(0) - 1)
    def _():
        for t in range(span):
            slot_copy(t).wait()

    for s in range(nt):
        acc_ref[:, s * LANES:(s + 1) * LANES] = acc3_ref[:, s, :]
    hh = h_ref[...] + acc_ref[...]
    ms = jnp.mean(hh * hh, axis=-1, keepdims=True)
    o_ref[...] = hh * lax.rsqrt(ms + RMS_EPS) * gain_ref[...]


def pack_expert_rows(u, v):
    n, d = u.shape
    nw = d // (2 * LANES)
    tn = 256

    def pack_kernel(u_ref, v_ref, o_ref):
        def bits(t):
            return lax.bitcast_convert_type(t.astype(BF16).astype(F32), jnp.uint32)

        for k, ref in enumerate((u_ref, v_ref)):
            for s in range(nw):
                lo = bits(ref[:, (2 * s) * LANES:(2 * s + 1) * LANES]) >> 16
                hi = bits(ref[:, (2 * s + 1) * LANES:(2 * s + 2) * LANES]) & jnp.uint32(0xFFFF0000)
                o_ref[:, k, s, :] = lo | hi

    return pl.pallas_call(
        pack_kernel,
        grid=(n // tn,),
        in_specs=[pl.BlockSpec((tn, d), lambda i: (i, 0))] * 2,
        out_specs=pl.BlockSpec((tn, 2, nw, LANES), lambda i: (i, 0, 0, 0)),
        out_shape=jax.ShapeDtypeStruct((n, 2, nw, LANES), jnp.uint32),
        compiler_params=_cparams(("parallel",)),
        name="peer_pack",
    )(u, v)


def peer_experts(idx, gate, xn, h, gain, uv, *, tb=64):
    M, D = xn.shape
    E = idx.shape[1]
    tb = min(tb, M)
    assert tb % PEER_SLOTS == 0 and M % tb == 0
    nb = M // tb
    idx3 = idx.reshape(nb, tb, E)
    ahead = jnp.concatenate([idx3[1:, :PEER_SLOTS], idx3[-1:, :PEER_SLOTS]], axis=0)
    idx_ext = jnp.concatenate([idx3, ahead], axis=1)
    spec = pl.BlockSpec((tb, D), lambda i: (i, 0))
    return pl.pallas_call(
        _peer_expert_kernel,
        grid=(nb,),
        in_specs=[
            pl.BlockSpec((1, tb + PEER_SLOTS, E), lambda i: (i, 0, 0), memory_space=pltpu.SMEM),
            pl.BlockSpec((tb, E), lambda i: (i, 0)),
            spec, spec,
            pl.BlockSpec((1, D), lambda i: (0, 0)),
            pl.BlockSpec(memory_space=pl.ANY),
        ],
        out_specs=spec,
        out_shape=jax.ShapeDtypeStruct((M, D), F32),
        scratch_shapes=[
            *[pltpu.VMEM((PEER_GROUP, 2, E, D // (2 * LANES), LANES), jnp.uint32)
              for _ in range(PEER_RING)],
            pltpu.VMEM((tb, D // LANES, LANES), F32),
            pltpu.VMEM((tb, D // LANES, LANES), F32),
            pltpu.VMEM((tb, D), F32),
            pltpu.SemaphoreType.DMA((PEER_SLOTS,)),
        ],
        compiler_params=_cparams(("arbitrary",)),
        name="peer_experts",
    )(idx_ext, gate, xn, h, gain.reshape(1, D), uv)


def _pack_w_in(w_in):
    D = w_in.shape[0]
    W = RWKV_WIDTH
    o = 0
    qk = w_in[:, o:o + 2 * MLSTM_QK]; o += 2 * MLSTM_QK
    v = w_in[:, o:o + MLSTM_V]; o += MLSTM_V
    og = w_in[:, o:o + MLSTM_V]; o += MLSTM_V
    ifg = w_in[:, o:o + 2 * MLSTM_HEADS]; o += 2 * MLSTM_HEADS
    rkv = w_in[:, o:o + 3 * W]; o += 3 * W
    wl = w_in[:, o:o + W_LORA]; o += W_LORA
    al = w_in[:, o:o + A_LORA]; o += A_LORA
    gl = w_in[:, o:o + G_LORA]; o += G_LORA
    gates = w_in[:, o:]

    def pad(t, n):
        return jnp.pad(t, ((0, 0), (0, n - t.shape[1])))

    w_mlstm = jnp.concatenate([qk, v, og, pad(ifg, LANES)], axis=1)
    w_rwkv = jnp.concatenate([rkv, pad(wl, LORA_PAD), pad(al, LORA_PAD), gl], axis=1)
    assert w_mlstm.shape == (D, MLSTM_PACKED) and w_rwkv.shape == (D, RWKV_PACKED)
    return w_mlstm.astype(BF16), w_rwkv.astype(BF16), gates.astype(BF16)


def kernel(x, norm_mix_gain, w_in, mlstm_conv, mlstm_b_i, mlstm_b_f, rwkv_mu, rwkv_w0, rwkv_w2,
           rwkv_a0, rwkv_a2, rwkv_g2, rwkv_k_k, rwkv_k_a, rwkv_r_k, rwkv_ln_w, rwkv_ln_b,
           proj_mlstm, proj_rwkv, w_out, norm_ffn_gain, peer_w_query, peer_sub_keys,
           peer_u, peer_v, norm_final_gain):
    B, T, D = x.shape
    assert w_in.shape[0] == 1, "the output norm is fused into the single layer's PEER kernel"
    l = 0
    x2d = x.reshape(B * T, D)
    w_mlstm, w_rwkv, w_gates = _pack_w_in(w_in[l])
    p_m = norm_matmul(x2d, norm_mix_gain[l], w_mlstm, tn=MLSTM_PACKED // 3).reshape(B, T, MLSTM_PACKED)
    p_r = norm_matmul(x2d, norm_mix_gain[l], w_rwkv, tn=RWKV_PACKED // 4).reshape(B, T, RWKV_PACKED)
    p_g = norm_matmul(x2d, norm_mix_gain[l], w_gates, tn=1024)

    y_m = mlstm_branch(p_m, mlstm_conv[l], mlstm_b_i[l], mlstm_b_f[l])
    r, w, k, v, aa, bb, g = rwkv_pre(p_r, rwkv_mu[l], rwkv_w0[l], rwkv_w2[l], rwkv_a0[l],
                                     rwkv_a2[l], rwkv_g2[l], rwkv_k_k[l], rwkv_k_a[l])
    y = rwkv_chunked(r, w, k, v, aa, bb)

    def flat(t):
        return t.reshape(B * T, RWKV_WIDTH)

    y_r = rwkv_post(flat(y), flat(r), flat(k), flat(v), flat(g),
                    rwkv_ln_w[l], rwkv_ln_b[l], rwkv_r_k[l])
    mixed = merge(y_m.reshape(B * T, MLSTM_V), y_r, p_g,
                  proj_mlstm[l].astype(BF16), proj_rwkv[l].astype(BF16))
    h2d, xn = out_proj(mixed, x2d, w_out[l].astype(BF16), norm_ffn_gain[l])
    q = matmul_split(xn, peer_w_query[l])
    idx, gate = peer_route(q, peer_sub_keys[l])
    out = peer_experts(idx, gate, xn, h2d, norm_final_gain, pack_expert_rows(peer_u[l], peer_v[l]))
    return out.reshape(B, T, D)
```

```python
import jax
import jax.numpy as jnp
from jax import lax
from jax.experimental import pallas as pl
from jax.experimental.pallas import tpu as pltpu

F32 = jnp.float32
BF16 = jnp.bfloat16

LANES = 128
SUBLANES = 8
VMEM_LIMIT_BYTES = 48 * 1024 * 1024

MLSTM_HEADS = 4
MLSTM_DK = 256
MLSTM_CONV = 4
MLSTM_CHUNK = 64
RWKV_HEADS = 16
RWKV_HEAD = 64
RWKV_WIDTH = RWKV_HEADS * RWKV_HEAD
W_LORA = 96
A_LORA = 96
G_LORA = 256
PEER_HEADS = 8
N_KEYS = 128
PEER_TOPK = 16
D_KEY = 256
RMS_EPS = 1e-6
GN_EPS = 64e-5
L2_EPS = 1e-12

PEER_CAND = PEER_TOPK + (SUBLANES - 1) * SUBLANES + (PEER_TOPK - SUBLANES)
PEER_SLOTS = 8
PEER_LOOKAHEAD = 6

MLSTM_QK = MLSTM_HEADS * MLSTM_DK
MLSTM_V = MLSTM_QK
LORA_PAD = 128

COL_QK = 0
COL_V = COL_QK + 2 * MLSTM_QK
COL_O = COL_V + MLSTM_V
COL_IF = COL_O + MLSTM_V
MLSTM_PACKED = COL_IF + LANES
RWKV_PACKED = 3 * RWKV_WIDTH + 2 * LORA_PAD + G_LORA


def _cparams(sem):
    return pltpu.CompilerParams(dimension_semantics=sem, vmem_limit_bytes=VMEM_LIMIT_BYTES)


def _split2(x):
    hi = x.astype(BF16)
    lo = (x - hi.astype(F32)).astype(BF16)
    return hi, lo


def _split3(x):
    hi = x.astype(BF16)
    r1 = x - hi.astype(F32)
    mid = r1.astype(BF16)
    lo = (r1 - mid.astype(F32)).astype(BF16)
    return hi, mid, lo


def _dot(a, b):
    return jnp.dot(a, b, preferred_element_type=F32)


def _dot_exact_rhs(x, ones_bf16):
    hi, mid, lo = _split3(x)
    return _dot(hi, ones_bf16) + _dot(mid, ones_bf16) + _dot(lo, ones_bf16)


def _norm_matmul_kernel(x_ref, g_ref, w_ref, o_ref, xn_ref):
    @pl.when(pl.program_id(1) == 0)
    def _():
        x = x_ref[...]
        ms = jnp.mean(x * x, axis=-1, keepdims=True)
        xn_ref[...] = (x * lax.rsqrt(ms + RMS_EPS) * g_ref[...]).astype(BF16)

    o_ref[...] = _dot(xn_ref[...], w_ref[...])


def norm_matmul(x, gain, w_bf16, *, tm=1024, tn=384):
    M, K = x.shape
    N = w_bf16.shape[1]
    tm = min(tm, M)
    return pl.pallas_call(
        _norm_matmul_kernel,
        grid=(M // tm, N // tn),
        in_specs=[
            pl.BlockSpec((tm, K), lambda i, j: (i, 0)),
            pl.BlockSpec((1, K), lambda i, j: (0, 0)),
            pl.BlockSpec((K, tn), lambda i, j: (0, j)),
        ],
        out_specs=pl.BlockSpec((tm, tn), lambda i, j: (i, j)),
        out_shape=jax.ShapeDtypeStruct((M, N), F32),
        scratch_shapes=[pltpu.VMEM((tm, K), BF16)],
        compiler_params=_cparams(("parallel", "arbitrary")),
        name="norm_matmul",
    )(x, gain.reshape(1, K), w_bf16)


def _mlstm_kernel(qk_ref, v_ref, o_ref, if_ref, conv_ref, bias_ref, y_ref,
                  ext_ref, c_ref, n_ref, m_ref):
    L = MLSTM_CHUNK
    DK = MLSTM_DK
    step = pl.program_id(1)

    @pl.when(step == 0)
    def _():
        ext_ref[0:SUBLANES, :] = jnp.zeros((SUBLANES, 2 * MLSTM_QK), F32)
        c_ref[...] = jnp.zeros_like(c_ref)
        n_ref[...] = jnp.zeros_like(n_ref)
        m_ref[...] = jnp.zeros_like(m_ref)

    ext_ref[SUBLANES:SUBLANES + L, :] = qk_ref[0]
    acc = jnp.zeros((L, 2 * MLSTM_QK), F32)
    for j in range(MLSTM_CONV):
        off = SUBLANES - (MLSTM_CONV - 1) + j
        acc = acc + ext_ref[off:off + L, :] * conv_ref[j:j + 1, :]
    ext_ref[0:SUBLANES, :] = qk_ref[0, L - SUBLANES:L, :]
    qk = acc * jax.nn.sigmoid(acc)

    row = lax.broadcasted_iota(jnp.int32, (L, L), 0)
    col = lax.broadcasted_iota(jnp.int32, (L, L), 1)
    causal = col <= row
    eye = col == row

    def to_row(x_col):
        return jnp.sum(jnp.where(eye, x_col, 0.0), axis=0, keepdims=True)

    gates = if_ref[0]
    for h in range(MLSTM_HEADS):
        q = qk[:, h * DK:(h + 1) * DK] * (DK ** -0.5)
        k = qk[:, MLSTM_QK + h * DK:MLSTM_QK + (h + 1) * DK]
        v = v_ref[0, :, h * DK:(h + 1) * DK]
        ig_col = gates[:, h:h + 1] + bias_ref[0:1, h:h + 1]
        lf_col = jax.nn.log_sigmoid(
            gates[:, MLSTM_HEADS + h:MLSTM_HEADS + h + 1]
            + bias_ref[1:2, h:h + 1])
        lf_row = to_row(lf_col)
        ig_row = to_row(ig_col)
        b_col = jnp.sum(jnp.where(causal, lf_row, 0.0), axis=1, keepdims=True)
        b_row = to_row(b_col)
        b_last = b_col[L - 1:L, :]
        m_prev = m_ref[h:h + 1, 0:1]
        C = c_ref[h]
        n_row = n_ref[h:h + 1, :]

        dmat = jnp.where(causal, b_col - b_row + ig_row, -jnp.inf)
        inter = b_col + m_prev
        m_t = jnp.maximum(inter, jnp.max(dmat, axis=1, keepdims=True))
        qb = q.astype(BF16)
        kb = k.astype(BF16)
        vb = v.astype(BF16)
        s = lax.dot_general(qb, kb, (((1,), (1,)), ((), ())),
                            preferred_element_type=F32) * jnp.exp(dmat - m_t)
        w_inter = jnp.exp(inter - m_t)
        num = _dot(s.astype(BF16), vb) + w_inter * _dot(qb, C.astype(BF16))
        qn = jnp.sum(q * n_row, axis=1, keepdims=True)
        den = jnp.sum(s, axis=1, keepdims=True) + w_inter * qn
        hh = num / jnp.maximum(jnp.abs(den), jnp.exp(-m_t))
        o = o_ref[0, :, h * DK:(h + 1) * DK]
        y_ref[0, :, h * DK:(h + 1) * DK] = (jax.nn.sigmoid(o) * hh).astype(y_ref.dtype)

        g_end = b_last - b_col + ig_col
        m_new = jnp.maximum(b_last + m_prev, jnp.max(g_end, axis=0, keepdims=True))
        decay = jnp.exp(b_last + m_prev - m_new)
        ws = jnp.exp(g_end - m_new)
        kw = k * ws
        c_ref[h] = decay * C + lax.dot_general(
            kw.astype(BF16), vb, (((0,), (0,)), ((), ())), preferred_element_type=F32)
        n_ref[h:h + 1, :] = decay * n_row + jnp.sum(kw, axis=0, keepdims=True)
        m_ref[h:h + 1, :] = jnp.broadcast_to(m_new, (1, LANES))


def mlstm_branch(p, conv_w, b_i, b_f):
    B, T, _ = p.shape
    L = MLSTM_CHUNK
    bias = jnp.zeros((SUBLANES, LANES), F32)
    bias = bias.at[0, :MLSTM_HEADS].set(b_i).at[1, :MLSTM_HEADS].set(b_f)
    nqk = 2 * MLSTM_QK
    return pl.pallas_call(
        _mlstm_kernel,
        grid=(B, T // L),
        in_specs=[
            pl.BlockSpec((1, L, nqk), lambda b, c: (b, c, COL_QK // nqk)),
            pl.BlockSpec((1, L, MLSTM_V), lambda b, c: (b, c, COL_V // MLSTM_V)),
            pl.BlockSpec((1, L, MLSTM_V), lambda b, c: (b, c, COL_O // MLSTM_V)),
            pl.BlockSpec((1, L, LANES), lambda b, c: (b, c, COL_IF // LANES)),
            pl.BlockSpec((MLSTM_CONV, nqk), lambda b, c: (0, 0)),
            pl.BlockSpec((SUBLANES, LANES), lambda b, c: (0, 0)),
        ],
        out_specs=pl.BlockSpec((1, L, MLSTM_V), lambda b, c: (b, c, 0)),
        out_shape=jax.ShapeDtypeStruct((B, T, MLSTM_V), BF16),
        scratch_shapes=[
            pltpu.VMEM((SUBLANES + L, nqk), F32),
            pltpu.VMEM((MLSTM_HEADS, MLSTM_DK, MLSTM_DK), F32),
            pltpu.VMEM((SUBLANES, MLSTM_DK), F32),
            pltpu.VMEM((SUBLANES, LANES), F32),
        ],
        compiler_params=_cparams(("parallel", "arbitrary")),
        name="mlstm",
    )(p, p, p, p, conv_w, bias)


def _head_sum_matrix(group):
    r = lax.broadcasted_iota(jnp.int32, (LANES, LANES), 0) // group
    c = lax.broadcasted_iota(jnp.int32, (LANES, LANES), 1) // group
    return jnp.where(r == c, 1.0, 0.0).astype(BF16)


def _rwkv_pre_kernel(p_ref, prev_ref, mu_ref, vec_ref, w2_ref, a2_ref, g2_ref,
                     r_ref, w_ref, k_ref, v_ref, aa_ref, bb_ref, g_ref):
    W = RWKV_WIDTH
    tb = p_ref.shape[1]
    p = p_ref[0]
    last = prev_ref[0, SUBLANES - 1:SUBLANES, :]
    last = jnp.where(pl.program_id(1) == 0, 0.0, last)
    rid = lax.broadcasted_iota(jnp.int32, (tb, 1), 0)
    shifted = jnp.where(rid == 0, last, pltpu.roll(p, 1, 0))
    p = p + (shifted - p) * mu_ref[...]

    r = p[:, 0:W]
    k = p[:, W:2 * W]
    v = p[:, 2 * W:3 * W]
    wl = p[:, 3 * W:3 * W + LORA_PAD]
    al = p[:, 3 * W + LORA_PAD:3 * W + 2 * LORA_PAD]
    gl = p[:, 3 * W + 2 * LORA_PAD:]
    w0 = vec_ref[0:1, :]
    a0 = vec_ref[1:2, :]
    k_k = vec_ref[2:3, :]
    k_a = vec_ref[3:4, :]

    wx = -(w0 + _dot(jnp.tanh(wl).astype(BF16), w2_ref[...]))
    softplus = jnp.maximum(wx, 0.0) + jnp.log1p(jnp.exp(-jnp.abs(wx)))
    w = -softplus - 0.5
    log_decay = -jnp.exp(w)
    a = jax.nn.sigmoid(a0 + _dot(al.astype(BF16), a2_ref[...]))
    g = _dot(jax.nn.sigmoid(gl).astype(BF16), g2_ref[...])

    kk = k * k_k
    ones = _head_sum_matrix(RWKV_HEAD)
    sq = kk * kk
    ss = jnp.concatenate(
        [_dot_exact_rhs(sq[:, j * LANES:(j + 1) * LANES], ones) for j in range(W // LANES)],
        axis=1)
    kk = kk / jnp.maximum(jnp.sqrt(ss), L2_EPS)

    r_ref[0] = r
    w_ref[0] = log_decay
    k_ref[0] = k * (1.0 + (a - 1.0) * k_a)
    v_ref[0] = v
    aa_ref[0] = -kk
    bb_ref[0] = kk * a
    g_ref[0] = g


def rwkv_pre(p, mu, w0, w2, a0, a2, g2, k_k, k_a, *, tb=256):
    B, T, _ = p.shape
    W = RWKV_WIDTH
    tb = min(tb, T)
    mu_p = jnp.zeros((1, RWKV_PACKED), F32)
    mu_p = mu_p.at[0, :3 * W].set(mu[:3 * W])
    mu_p = mu_p.at[0, 3 * W:3 * W + W_LORA].set(mu[3 * W:3 * W + W_LORA])
    mu_p = mu_p.at[0, 3 * W + LORA_PAD:3 * W + LORA_PAD + A_LORA].set(
        mu[3 * W + W_LORA:3 * W + W_LORA + A_LORA])
    mu_p = mu_p.at[0, 3 * W + 2 * LORA_PAD:].set(mu[3 * W + W_LORA + A_LORA:])
    vec = jnp.zeros((SUBLANES, W), F32)
    vec = vec.at[0].set(w0).at[1].set(a0).at[2].set(k_k).at[3].set(k_a)
    w2_p = jnp.zeros((LORA_PAD, W), F32).at[:W_LORA].set(w2).astype(BF16)
    a2_p = jnp.zeros((LORA_PAD, W), F32).at[:A_LORA].set(a2).astype(BF16)
    nprev = tb // SUBLANES
    out = jax.ShapeDtypeStruct((B, T, W), F32)
    ospec = pl.BlockSpec((1, tb, W), lambda b, i: (b, i, 0))
    return pl.pallas_call(
        _rwkv_pre_kernel,
        grid=(B, T // tb),
        in_specs=[
            pl.BlockSpec((1, tb, RWKV_PACKED), lambda b, i: (b, i, 0)),
            pl.BlockSpec((1, SUBLANES, RWKV_PACKED),
                         lambda b, i: (b, jnp.maximum(i * nprev - 1, 0), 0)),
            pl.BlockSpec((1, RWKV_PACKED), lambda b, i: (0, 0)),
            pl.BlockSpec((SUBLANES, W), lambda b, i: (0, 0)),
            pl.BlockSpec((LORA_PAD, W), lambda b, i: (0, 0)),
            pl.BlockSpec((LORA_PAD, W), lambda b, i: (0, 0)),
            pl.BlockSpec((G_LORA, W), lambda b, i: (0, 0)),
        ],
        out_specs=[ospec] * 7,
        out_shape=[out] * 7,
        compiler_params=_cparams(("parallel", "parallel")),
        name="rwkv_pre",
    )(p, p, mu_p, vec, w2_p, a2_p, g2.astype(BF16))


RWKV_CHUNK = 64


def _mm3(a, b, dims):
    ah, al = _split2(a)
    bh, bl = _split2(b)

    def dg(x, y):
        return lax.dot_general(x, y, (dims, ((), ())), preferred_element_type=F32)

    return dg(ah, bh) + dg(al, bh) + dg(ah, bl)


_NN = ((1,), (0,))
_NT = ((1,), (1,))
_TN = ((0,), (0,))


def _rwkv_chunk_kernel(r_ref, lw_ref, k_ref, v_ref, aa_ref, bb_ref, q_ref, y0_ref, p_ref, g_ref):
    L = RWKV_CHUNK
    N = RWKV_HEAD
    W = RWKV_WIDTH
    npair = W // LANES
    row = lax.broadcasted_iota(jnp.int32, (L, LANES), 0)
    lane = lax.broadcasted_iota(jnp.int32, (L, LANES), 1)
    first = lane < N
    pos = lane % N
    strict = pos < row
    lower = pos <= row
    eye = pos == row

    def bd(x):
        return jnp.concatenate([jnp.where(first, x, 0.0), jnp.where(first, 0.0, x)], axis=0)

    def unbd(x):
        return jnp.where(first, x[0:L], x[L:2 * L])

    tri = jnp.where(lax.broadcasted_iota(jnp.int32, (L, L), 1)
                    <= lax.broadcasted_iota(jnp.int32, (L, L), 0), 1.0, 0.0).astype(BF16)
    lw = lw_ref[0]
    c = _dot_exact_rhs_left(tri, lw)
    c_last = c[L - 1:L, :]
    e_pos = jnp.exp(c)
    e_neg = jnp.exp(-c)
    e_hat = jnp.exp(c_last - c)
    a_all = aa_ref[0]
    b_all = bb_ref[0]
    k_all = k_ref[0]
    at_all = a_all * jnp.exp(c - lw)
    rt_all = r_ref[0] * e_pos
    bt_all = b_all * e_neg
    kt_all = k_all * e_neg
    bh_all = b_all * e_hat
    kh_all = k_all * e_hat
    gl_all = jnp.exp(c_last)

    def pair(x, q):
        return x[:, q * LANES:(q + 1) * LANES]

    pairs = range(npair)
    at = [pair(at_all, q) for q in pairs]
    rt = [pair(rt_all, q) for q in pairs]
    v = [pair(v_ref[0], q) for q in pairs]
    bt_bd = [bd(pair(bt_all, q)) for q in pairs]
    kt_bd = [bd(pair(kt_all, q)) for q in pairs]
    v_bd = [bd(x) for x in v]

    ar = [jnp.concatenate([at[q], rt[q]], axis=0) for q in pairs]
    sb = [_mm3(ar[q], bt_bd[q], _NT) for q in pairs]
    sk = [_mm3(ar[q], kt_bd[q], _NT) for q in pairs]
    aab = [jnp.where(strict, sb[q][0:L], 0.0) for q in pairs]
    mrb = [jnp.where(lower, sb[q][L:2 * L], 0.0) for q in pairs]
    aak_mrk = [jnp.concatenate([jnp.where(strict, sk[q][0:L], 0.0),
                                jnp.where(lower, sk[q][L:2 * L], 0.0)], axis=0) for q in pairs]

    t = [jnp.where(eye, 1.0, 0.0) + aab[q] for q in pairs]
    x = [_mm3(aab[q], bd(aab[q]), _NN) for q in pairs]
    n = 2
    while 2 * n < L:
        tx = [_mm3(jnp.concatenate([t[q], x[q]], axis=0), bd(x[q]), _NN) for q in pairs]
        t = [t[q] + tx[q][0:L] for q in pairs]
        x = [tx[q][L:2 * L] for q in pairs]
        n *= 2
    t = [t[q] + _mm3(t[q], bd(x[q]), _NN) for q in pairs]

    av_mv = [_mm3(aak_mrk[q], v_bd[q], _NN) for q in pairs]
    wu = [_mm3(t[q], jnp.concatenate([bd(at[q]), bd(av_mv[q][0:L])], axis=1), _NN)
          for q in pairs]
    w = [wu[q][:, 0:LANES] for q in pairs]
    u0 = [wu[q][:, LANES:2 * LANES] for q in pairs]
    qy = [_mm3(mrb[q], jnp.concatenate([bd(w[q]), bd(u0[q])], axis=1), _NN) for q in pairs]
    qq = [rt[q] + qy[q][:, 0:LANES] for q in pairs]
    y0 = [qy[q][:, LANES:2 * LANES] + av_mv[q][L:2 * L] for q in pairs]
    bh = [pair(bh_all, q) for q in pairs]
    kh = [pair(kh_all, q) for q in pairs]
    pg = [_mm3(bh[q], wu[q], _TN) for q in pairs]
    kv = [_mm3(kh[q], v[q], _TN) for q in pairs]
    pp = [unbd(pg[q][:, 0:LANES]) + jnp.where(eye, pair(gl_all, q), 0.0) for q in pairs]
    gg = [unbd(pg[q][:, LANES:2 * LANES]) + unbd(kv[q]) for q in pairs]
    for q in pairs:
        sl = slice(q * LANES, (q + 1) * LANES)
        q_ref[0, :, sl] = qq[q]
        y0_ref[0, :, sl] = y0[q]
        p_ref[0, :, sl] = pp[q]
        g_ref[0, :, sl] = gg[q]


def _dot_exact_rhs_left(ones_bf16, x):
    hi, mid, lo = _split3(x)
    return _dot(ones_bf16, hi) + _dot(ones_bf16, mid) + _dot(ones_bf16, lo)


def _rwkv_state_kernel(q_ref, y0_ref, p_ref, g_ref, y_ref, h_ref):
    L = RWKV_CHUNK
    N = RWKV_HEAD
    npair = RWKV_WIDTH // LANES

    @pl.when(pl.program_id(1) == 0)
    def _():
        h_ref[...] = jnp.zeros_like(h_ref)

    first = lax.broadcasted_iota(jnp.int32, (N, LANES), 1) < N

    def bd(x):
        return jnp.concatenate([jnp.where(first, x, 0.0), jnp.where(first, 0.0, x)], axis=0)

    pairs = range(npair)
    sls = [slice(q * LANES, (q + 1) * LANES) for q in pairs]
    h_bd = [bd(h_ref[:, sls[q]]) for q in pairs]
    y = [_mm3(q_ref[0, :, sls[q]], h_bd[q], _NN) for q in pairs]
    hn = [_mm3(p_ref[0, :, sls[q]], h_bd[q], _NN) for q in pairs]
    for q in pairs:
        y_ref[0, :, sls[q]] = y[q] + y0_ref[0, :, sls[q]]
        h_ref[:, sls[q]] = hn[q] + g_ref[0, :, sls[q]]


def rwkv_chunked(r, lw, k, v, aa, bb):
    B, T, W = r.shape
    L = RWKV_CHUNK
    spec = pl.BlockSpec((1, L, W), lambda b, c: (b, c, 0))
    out = jax.ShapeDtypeStruct((B, T, W), F32)
    q, y0, p, g = pl.pallas_call(
        _rwkv_chunk_kernel,
        grid=(B, T // L),
        in_specs=[spec] * 6,
        out_specs=[spec] * 4,
        out_shape=[out] * 4,
        compiler_params=_cparams(("parallel", "parallel")),
        name="rwkv_chunk",
    )(r, lw, k, v, aa, bb)
    return pl.pallas_call(
        _rwkv_state_kernel,
        grid=(B, T // L),
        in_specs=[spec] * 4,
        out_specs=spec,
        out_shape=out,
        scratch_shapes=[pltpu.VMEM((RWKV_HEAD, W), F32)],
        compiler_params=_cparams(("parallel", "arbitrary")),
        name="rwkv_state",
    )(q, y0, p, g)


def _rwkv_post_kernel(y_ref, r_ref, k_ref, v_ref, g_ref, vec_ref, o_ref):
    W = RWKV_WIDTH
    ones = _head_sum_matrix(RWKV_HEAD)
    ln_w = vec_ref[0:1, :]
    ln_b = vec_ref[1:2, :]
    r_k = vec_ref[2:3, :]

    def head_sum(x):
        return jnp.concatenate(
            [_dot_exact_rhs(x[:, j * LANES:(j + 1) * LANES], ones) for j in range(W // LANES)],
            axis=1)

    y = y_ref[...]
    mean = head_sum(y) * (1.0 / RWKV_HEAD)
    d = y - mean
    var = head_sum(d * d) * (1.0 / RWKV_HEAD)
    yn = d * lax.rsqrt(var + GN_EPS) * ln_w + ln_b
    bonus = head_sum(r_ref[...] * k_ref[...] * r_k) * v_ref[...]
    o_ref[...] = ((yn + bonus) * g_ref[...]).astype(o_ref.dtype)


def rwkv_post(y, r, k, v, g, ln_w, ln_b, r_k, *, tb=256):
    M, W = y.shape
    tb = min(tb, M)
    vec = jnp.zeros((SUBLANES, W), F32)
    vec = vec.at[0].set(ln_w).at[1].set(ln_b).at[2].set(r_k.reshape(W))
    spec = pl.BlockSpec((tb, W), lambda i: (i, 0))
    return pl.pallas_call(
        _rwkv_post_kernel,
        grid=(M // tb,),
        in_specs=[spec] * 5 + [pl.BlockSpec((SUBLANES, W), lambda i: (0, 0))],
        out_specs=spec,
        out_shape=jax.ShapeDtypeStruct((M, W), BF16),
        compiler_params=_cparams(("parallel",)),
        name="rwkv_post",
    )(y, r, k, v, g, vec)


def _merge_kernel(ym_ref, yr_ref, gm_ref, gr_ref, pm_ref, pr_ref, o_ref):
    m = _dot(ym_ref[...], pm_ref[...])
    r = _dot(yr_ref[...], pr_ref[...])
    o_ref[...] = (jax.nn.sigmoid(gm_ref[...]) * m
                  + jax.nn.sigmoid(gr_ref[...]) * r).astype(o_ref.dtype)


def merge(y_m, y_r, gates, proj_m, proj_r, *, tm=512, tn=2048):
    M, K = y_m.shape
    D = proj_m.shape[1]
    tm = min(tm, M)
    return pl.pallas_call(
        _merge_kernel,
        grid=(M // tm, D // tn),
        in_specs=[
            pl.BlockSpec((tm, K), lambda i, j: (i, 0)),
            pl.BlockSpec((tm, K), lambda i, j: (i, 0)),
            pl.BlockSpec((tm, tn), lambda i, j: (i, j)),
            pl.BlockSpec((tm, tn), lambda i, j: (i, D // tn + j)),
            pl.BlockSpec((K, tn), lambda i, j: (0, j)),
            pl.BlockSpec((K, tn), lambda i, j: (0, j)),
        ],
        out_specs=pl.BlockSpec((tm, tn), lambda i, j: (i, j)),
        out_shape=jax.ShapeDtypeStruct((M, D), BF16),
        compiler_params=_cparams(("parallel", "arbitrary")),
        name="merge",
    )(y_m, y_r, gates, gates, proj_m, proj_r)


def _out_proj_kernel(mix_ref, x_ref, w_ref, g_ref, h_ref, xn_ref):
    h = x_ref[...] + _dot(mix_ref[...], w_ref[...])
    h_ref[...] = h
    ms = jnp.mean(h * h, axis=-1, keepdims=True)
    xn_ref[...] = h * lax.rsqrt(ms + RMS_EPS) * g_ref[...]


def out_proj(mixed, x2d, w_out_bf16, gain, *, tm=256):
    M, D = x2d.shape
    tm = min(tm, M)
    spec = pl.BlockSpec((tm, D), lambda i: (i, 0))
    return pl.pallas_call(
        _out_proj_kernel,
        grid=(M // tm,),
        in_specs=[spec, spec, pl.BlockSpec((D, D), lambda i: (0, 0)),
                  pl.BlockSpec((1, D), lambda i: (0, 0))],
        out_specs=[spec, spec],
        out_shape=[jax.ShapeDtypeStruct((M, D), F32)] * 2,
        compiler_params=_cparams(("parallel",)),
        name="out_proj",
    )(mixed, x2d, w_out_bf16, gain.reshape(1, D))


def _matmul_split_kernel(a_ref, whi_ref, wlo_ref, o_ref, hi_ref, lo_ref):
    @pl.when(pl.program_id(1) == 0)
    def _():
        hi, lo = _split2(a_ref[...])
        hi_ref[...] = hi
        lo_ref[...] = lo

    o_ref[...] = (_dot(hi_ref[...], whi_ref[...]) + _dot(lo_ref[...], whi_ref[...])
                  + _dot(hi_ref[...], wlo_ref[...]))


def matmul_split(a, w, *, tm=512, tn=512):
    M, K = a.shape
    N = w.shape[1]
    tm = min(tm, M)
    whi = w.astype(BF16)
    wlo = (w - whi.astype(F32)).astype(BF16)
    return pl.pallas_call(
        _matmul_split_kernel,
        grid=(M // tm, N // tn),
        in_specs=[pl.BlockSpec((tm, K), lambda i, j: (i, 0)),
                  pl.BlockSpec((K, tn), lambda i, j: (0, j)),
                  pl.BlockSpec((K, tn), lambda i, j: (0, j))],
        out_specs=pl.BlockSpec((tm, tn), lambda i, j: (i, j)),
        out_shape=jax.ShapeDtypeStruct((M, N), F32),
        scratch_shapes=[pltpu.VMEM((tm, K), BF16), pltpu.VMEM((tm, K), BF16)],
        compiler_params=_cparams(("parallel", "arbitrary")),
        name="peer_query",
    )(a, whi, wlo)


def _route_kernel(q_ref, khi_ref, klo_ref, idx_ref, gate_ref,
                  s_ref, tv_ref, tp_ref, c_ref, e_ref, bv_ref, bi_ref):
    tn = q_ref.shape[0]
    KK = PEER_TOPK
    half = D_KEY // 2
    NEG = -jnp.inf
    key_id = lax.broadcasted_iota(jnp.int32, (N_KEYS, tn), 0).astype(F32)
    crow = lax.broadcasted_iota(jnp.int32, (PEER_CAND, tn), 0)
    mid = crow - KK
    cand_id = jnp.where(
        crow < KK, crow,
        jnp.where(crow < PEER_CAND - SUBLANES,
                  (mid // SUBLANES + 1) * KK + mid % SUBLANES,
                  (crow - (PEER_CAND - SUBLANES) + SUBLANES) * KK)).astype(F32)
    rank = lax.broadcasted_iota(jnp.int32, (KK, tn), 0)

    def nt(a, b):
        return lax.dot_general(a, b, (((1,), (1,)), ((), ())), preferred_element_type=F32)

    for hp in range(2 * PEER_HEADS):
        qhi, qlo = _split2(q_ref[:, hp * half:(hp + 1) * half])
        p = hp % 2
        s_ref[hp] = nt(khi_ref[p], qhi) + nt(khi_ref[p], qlo) + nt(klo_ref[p], qhi)
    tv_ref[...] = jnp.zeros_like(tv_ref)
    tp_ref[...] = jnp.zeros_like(tp_ref)
    bv_ref[...] = jnp.zeros_like(bv_ref)
    bi_ref[...] = jnp.zeros_like(bi_ref)

    def sub_topk(kk, carry):
        for hp in range(2 * PEER_HEADS):
            s = s_ref[hp]
            m = jnp.max(s, axis=0, keepdims=True)
            pos = jnp.min(jnp.where(s == m, key_id, float(N_KEYS)), axis=0, keepdims=True)
            s_ref[hp] = jnp.where(key_id == pos, NEG, s)
            tv_ref[hp] = jnp.where(rank == kk, m, tv_ref[hp])
            tp_ref[hp] = jnp.where(rank == kk, pos, tp_ref[hp])
        return carry

    lax.fori_loop(0, KK, sub_topk, 0)

    grp = lax.broadcasted_iota(jnp.int32, (SUBLANES, tn), 0)
    for h in range(PEER_HEADS):
        v1, v2 = tv_ref[2 * h], tv_ref[2 * h + 1]
        p1, p2 = tp_ref[2 * h], tp_ref[2 * h + 1]
        c_ref[h, 0:KK, :] = v1[0:1, :] + v2
        e_ref[h, 0:KK, :] = p1[0:1, :] * float(N_KEYS) + p2
        for i in range(1, SUBLANES):
            keep = grp < KK // (i + 1)
            rows = slice(KK + (i - 1) * SUBLANES, KK + i * SUBLANES)
            c_ref[h, rows, :] = jnp.where(keep, v1[i:i + 1, :] + v2[0:SUBLANES, :], NEG)
            e_ref[h, rows, :] = jnp.where(
                keep, p1[i:i + 1, :] * float(N_KEYS) + p2[0:SUBLANES, :], -1.0)
        c_ref[h, PEER_CAND - SUBLANES:PEER_CAND, :] = v1[SUBLANES:KK, :] + v2[0:1, :]
        e_ref[h, PEER_CAND - SUBLANES:PEER_CAND, :] = p1[SUBLANES:KK, :] * float(N_KEYS) + p2[0:1, :]

    def cand_topk(kk, carry):
        for h in range(PEER_HEADS):
            c = c_ref[h]
            m = jnp.max(c, axis=0, keepdims=True)
            pos = jnp.min(jnp.where(c == m, cand_id, float(KK * KK)), axis=0, keepdims=True)
            hit = cand_id == pos
            eid = jnp.max(jnp.where(hit, e_ref[h], -1.0), axis=0, keepdims=True)
            c_ref[h] = jnp.where(hit, NEG, c)
            rows = slice(h * KK, (h + 1) * KK)
            bv_ref[rows, :] = jnp.where(rank == kk, m, bv_ref[rows, :])
            bi_ref[rows, :] = jnp.where(rank == kk, eid, bi_ref[rows, :])
        return carry

    lax.fori_loop(0, KK, cand_topk, 0)

    for h in range(PEER_HEADS):
        rows = slice(h * KK, (h + 1) * KK)
        b = bv_ref[rows, :]
        e = jnp.exp(b - b[0:1, :])
        bv_ref[rows, :] = e / jnp.sum(e, axis=0, keepdims=True)
    gate_ref[...] = bv_ref[...].T
    idx_ref[...] = bi_ref[...].T.astype(jnp.int32)


def peer_route(q, sub_keys, *, tn=128):
    M = q.shape[0]
    tn = min(tn, M)
    khi = sub_keys.astype(BF16)
    klo = (sub_keys - khi.astype(F32)).astype(BF16)
    kspec = pl.BlockSpec((2, N_KEYS, D_KEY // 2), lambda i: (0, 0, 0))
    ospec = pl.BlockSpec((tn, N_KEYS), lambda i: (i, 0))
    return pl.pallas_call(
        _route_kernel,
        grid=(M // tn,),
        in_specs=[pl.BlockSpec((tn, PEER_HEADS * D_KEY), lambda i: (i, 0)), kspec, kspec],
        out_specs=[ospec, ospec],
        out_shape=[jax.ShapeDtypeStruct((M, N_KEYS), jnp.int32),
                   jax.ShapeDtypeStruct((M, N_KEYS), F32)],
        scratch_shapes=[
            pltpu.VMEM((2 * PEER_HEADS, N_KEYS, tn), F32),
            pltpu.VMEM((2 * PEER_HEADS, PEER_TOPK, tn), F32),
            pltpu.VMEM((2 * PEER_HEADS, PEER_TOPK, tn), F32),
            pltpu.VMEM((PEER_HEADS, PEER_CAND, tn), F32),
            pltpu.VMEM((PEER_HEADS, PEER_CAND, tn), F32),
            pltpu.VMEM((PEER_HEADS * PEER_TOPK, tn), F32),
            pltpu.VMEM((PEER_HEADS * PEER_TOPK, tn), F32),
        ],
        compiler_params=_cparams(("parallel",)),
        name="peer_route",
    )(q, khi, klo)


def _gelu_exact(x):
    return 0.5 * x * (1.0 + lax.erf(x * (2.0 ** -0.5)))


def _peer_expert_kernel(idx_ref, gate_ref, xn_ref, h_ref, gain_ref, uv_ref, o_ref,
                        buf_even_ref, buf_odd_ref, acc_ref, sem):
    tb, D = xn_ref.shape
    E = idx_ref.shape[2]
    nt = D // LANES
    step = pl.program_id(0)

    def staging(slot):
        return (buf_odd_ref if slot % 2 else buf_even_ref).at[slot // 2]

    def slot_copy(slot):
        return pltpu.make_async_copy(staging(slot), staging(slot), sem.at[slot])

    def issue(t, slot):
        for e in range(E):
            pltpu.make_async_copy(uv_ref.at[idx_ref[0, t, e]], staging(slot).at[:, e, :],
                                  sem.at[slot]).start(priority=e % 2)

    @pl.when(step == 0)
    def _():
        for t in range(PEER_LOOKAHEAD):
            issue(t, t)

    eye = (lax.broadcasted_iota(jnp.int32, (E, E), 0)
           == lax.broadcasted_iota(jnp.int32, (E, E), 1))

    def body(t8, carry):
        base = pl.multiple_of(t8 * SUBLANES, SUBLANES)
        x8 = xn_ref[pl.ds(base, SUBLANES), :]
        g8 = gate_ref[pl.ds(base, SUBLANES), :]
        rows = []
        for j in range(SUBLANES):
            slot = j % PEER_SLOTS
            issue(base + j + PEER_LOOKAHEAD, (j + PEER_LOOKAHEAD) % PEER_SLOTS)
            slot_copy(slot).wait()
            part = jnp.zeros((E, LANES), F32)
            for s in range(nt):
                part = part + staging(slot)[s] * x8[j:j + 1, s * LANES:(s + 1) * LANES]
            act = jnp.sum(part, axis=1, keepdims=True)
            g_col = jnp.sum(jnp.where(eye, g8[j:j + 1], 0.0), axis=1, keepdims=True)
            c = g_col * _gelu_exact(act)
            rows.append(jnp.concatenate(
                [jnp.sum(staging(slot)[nt + s] * c, axis=0, keepdims=True) for s in range(nt)],
                axis=1))
        acc_ref[pl.ds(base, SUBLANES), :] = jnp.concatenate(rows, axis=0)
        return carry

    lax.fori_loop(0, tb // SUBLANES, body, 0)

    @pl.when(step == pl.num_programs(0) - 1)
    def _():
        for t in range(PEER_LOOKAHEAD):
            slot_copy(t % PEER_SLOTS).wait()

    hh = h_ref[...] + acc_ref[...]
    ms = jnp.mean(hh * hh, axis=-1, keepdims=True)
    o_ref[...] = hh * lax.rsqrt(ms + RMS_EPS) * gain_ref[...]


def pack_expert_rows(u, v):
    n, d = u.shape
    nt = d // LANES
    tn = 256

    def pack_kernel(u_ref, v_ref, o_ref):
        for s in range(nt):
            o_ref[:, s, :] = u_ref[:, s * LANES:(s + 1) * LANES]
            o_ref[:, nt + s, :] = v_ref[:, s * LANES:(s + 1) * LANES]

    return pl.pallas_call(
        pack_kernel,
        grid=(n // tn,),
        in_specs=[pl.BlockSpec((tn, d), lambda i: (i, 0))] * 2,
        out_specs=pl.BlockSpec((tn, 2 * nt, LANES), lambda i: (i, 0, 0)),
        out_shape=jax.ShapeDtypeStruct((n, 2 * nt, LANES), F32),
        compiler_params=_cparams(("parallel",)),
        name="peer_pack",
    )(u, v)


def peer_experts(idx, gate, xn, h, gain, uv, *, tb=64):
    M, D = xn.shape
    E = idx.shape[1]
    tb = min(tb, M)
    assert tb % PEER_SLOTS == 0 and M % tb == 0
    nb = M // tb
    idx3 = idx.reshape(nb, tb, E)
    ahead = jnp.concatenate([idx3[1:, :PEER_SLOTS], idx3[-1:, :PEER_SLOTS]], axis=0)
    idx_ext = jnp.concatenate([idx3, ahead], axis=1)
    spec = pl.BlockSpec((tb, D), lambda i: (i, 0))
    return pl.pallas_call(
        _peer_expert_kernel,
        grid=(nb,),
        in_specs=[
            pl.BlockSpec((1, tb + PEER_SLOTS, E), lambda i: (i, 0, 0), memory_space=pltpu.SMEM),
            pl.BlockSpec((tb, E), lambda i: (i, 0)),
            spec, spec,
            pl.BlockSpec((1, D), lambda i: (0, 0)),
            pl.BlockSpec(memory_space=pl.ANY),
        ],
        out_specs=spec,
        out_shape=jax.ShapeDtypeStruct((M, D), F32),
        scratch_shapes=[
            pltpu.VMEM((PEER_SLOTS // 2, 2 * D // LANES, E, LANES), F32),
            pltpu.VMEM((PEER_SLOTS // 2, 2 * D // LANES, E, LANES), F32),
            pltpu.VMEM((tb, D), F32),
            pltpu.SemaphoreType.DMA((PEER_SLOTS,)),
        ],
        compiler_params=_cparams(("arbitrary",)),
        name="peer_experts",
    )(idx_ext, gate, xn, h, gain.reshape(1, D), uv)


def _pack_w_in(w_in):
    D = w_in.shape[0]
    W = RWKV_WIDTH
    o = 0
    qk = w_in[:, o:o + 2 * MLSTM_QK]; o += 2 * MLSTM_QK
    v = w_in[:, o:o + MLSTM_V]; o += MLSTM_V
    og = w_in[:, o:o + MLSTM_V]; o += MLSTM_V
    ifg = w_in[:, o:o + 2 * MLSTM_HEADS]; o += 2 * MLSTM_HEADS
    rkv = w_in[:, o:o + 3 * W]; o += 3 * W
    wl = w_in[:, o:o + W_LORA]; o += W_LORA
    al = w_in[:, o:o + A_LORA]; o += A_LORA
    gl = w_in[:, o:o + G_LORA]; o += G_LORA
    gates = w_in[:, o:]

    def pad(t, n):
        return jnp.pad(t, ((0, 0), (0, n - t.shape[1])))

    w_mlstm = jnp.concatenate([qk, v, og, pad(ifg, LANES)], axis=1)
    w_rwkv = jnp.concatenate([rkv, pad(wl, LORA_PAD), pad(al, LORA_PAD), gl], axis=1)
    assert w_mlstm.shape == (D, MLSTM_PACKED) and w_rwkv.shape == (D, RWKV_PACKED)
    return w_mlstm.astype(BF16), w_rwkv.astype(BF16), gates.astype(BF16)


def kernel(x, norm_mix_gain, w_in, mlstm_conv, mlstm_b_i, mlstm_b_f, rwkv_mu, rwkv_w0, rwkv_w2,
           rwkv_a0, rwkv_a2, rwkv_g2, rwkv_k_k, rwkv_k_a, rwkv_r_k, rwkv_ln_w, rwkv_ln_b,
           proj_mlstm, proj_rwkv, w_out, norm_ffn_gain, peer_w_query, peer_sub_keys,
           peer_u, peer_v, norm_final_gain):
    B, T, D = x.shape
    assert w_in.shape[0] == 1, "the output norm is fused into the single layer's PEER kernel"
    l = 0
    x2d = x.reshape(B * T, D)
    w_mlstm, w_rwkv, w_gates = _pack_w_in(w_in[l])
    p_m = norm_matmul(x2d, norm_mix_gain[l], w_mlstm, tn=MLSTM_PACKED // 3).reshape(B, T, MLSTM_PACKED)
    p_r = norm_matmul(x2d, norm_mix_gain[l], w_rwkv, tn=RWKV_PACKED // 4).reshape(B, T, RWKV_PACKED)
    p_g = norm_matmul(x2d, norm_mix_gain[l], w_gates, tn=1024)

    y_m = mlstm_branch(p_m, mlstm_conv[l], mlstm_b_i[l], mlstm_b_f[l])
    r, w, k, v, aa, bb, g = rwkv_pre(p_r, rwkv_mu[l], rwkv_w0[l], rwkv_w2[l], rwkv_a0[l],
                                     rwkv_a2[l], rwkv_g2[l], rwkv_k_k[l], rwkv_k_a[l])
    y = rwkv_chunked(r, w, k, v, aa, bb)

    def flat(t):
        return t.reshape(B * T, RWKV_WIDTH)

    y_r = rwkv_post(flat(y), flat(r), flat(k), flat(v), flat(g),
                    rwkv_ln_w[l], rwkv_ln_b[l], rwkv_r_k[l])
    mixed = merge(y_m.reshape(B * T, MLSTM_V), y_r, p_g,
                  proj_mlstm[l].astype(BF16), proj_rwkv[l].astype(BF16))
    h2d, xn = out_proj(mixed, x2d, w_out[l].astype(BF16), norm_ffn_gain[l])
    q = matmul_split(xn, peer_w_query[l])
    idx, gate = peer_route(q, peer_sub_keys[l])
    out = peer_experts(idx, gate, xn, h2d, norm_final_gain, pack_expert_rows(peer_u[l], peer_v[l]))
    return out.reshape(B, T, D)
```

```python
import jax
import jax.numpy as jnp
from jax import lax
from jax.experimental import pallas as pl
from jax.experimental.pallas import tpu as pltpu

F32 = jnp.float32
BF16 = jnp.bfloat16

LANES = 128
SUBLANES = 8
VMEM_LIMIT_BYTES = 48 * 1024 * 1024

MLSTM_HEADS = 4
MLSTM_DK = 256
MLSTM_CONV = 4
MLSTM_CHUNK = 64
RWKV_HEADS = 16
RWKV_HEAD = 64
RWKV_WIDTH = RWKV_HEADS * RWKV_HEAD
W_LORA = 96
A_LORA = 96
G_LORA = 256
PEER_HEADS = 8
N_KEYS = 128
PEER_TOPK = 16
D_KEY = 256
RMS_EPS = 1e-6
GN_EPS = 64e-5
L2_EPS = 1e-12

PEER_CAND = PEER_TOPK + (SUBLANES - 1) * SUBLANES + (PEER_TOPK - SUBLANES)
PEER_SLOTS = 8
PEER_LOOKAHEAD = 6

MLSTM_QK = MLSTM_HEADS * MLSTM_DK
MLSTM_V = MLSTM_QK
LORA_PAD = 128

COL_QK = 0
COL_V = COL_QK + 2 * MLSTM_QK
COL_O = COL_V + MLSTM_V
COL_IF = COL_O + MLSTM_V
MLSTM_PACKED = COL_IF + LANES
RWKV_PACKED = 3 * RWKV_WIDTH + 2 * LORA_PAD + G_LORA


def _cparams(sem):
    return pltpu.CompilerParams(dimension_semantics=sem, vmem_limit_bytes=VMEM_LIMIT_BYTES)


def _split2(x):
    hi = x.astype(BF16)
    lo = (x - hi.astype(F32)).astype(BF16)
    return hi, lo


def _split3(x):
    hi = x.astype(BF16)
    r1 = x - hi.astype(F32)
    mid = r1.astype(BF16)
    lo = (r1 - mid.astype(F32)).astype(BF16)
    return hi, mid, lo


def _dot(a, b):
    return jnp.dot(a, b, preferred_element_type=F32)


def _dot_exact_rhs(x, ones_bf16):
    hi, mid, lo = _split3(x)
    return _dot(hi, ones_bf16) + _dot(mid, ones_bf16) + _dot(lo, ones_bf16)


def _norm_matmul_kernel(x_ref, g_ref, w_ref, o_ref, xn_ref):
    @pl.when(pl.program_id(1) == 0)
    def _():
        x = x_ref[...]
        ms = jnp.mean(x * x, axis=-1, keepdims=True)
        xn_ref[...] = (x * lax.rsqrt(ms + RMS_EPS) * g_ref[...]).astype(BF16)

    o_ref[...] = _dot(xn_ref[...], w_ref[...])


def norm_matmul(x, gain, w_bf16, *, tm=1024, tn=384):
    M, K = x.shape
    N = w_bf16.shape[1]
    tm = min(tm, M)
    return pl.pallas_call(
        _norm_matmul_kernel,
        grid=(M // tm, N // tn),
        in_specs=[
            pl.BlockSpec((tm, K), lambda i, j: (i, 0)),
            pl.BlockSpec((1, K), lambda i, j: (0, 0)),
            pl.BlockSpec((K, tn), lambda i, j: (0, j)),
        ],
        out_specs=pl.BlockSpec((tm, tn), lambda i, j: (i, j)),
        out_shape=jax.ShapeDtypeStruct((M, N), F32),
        scratch_shapes=[pltpu.VMEM((tm, K), BF16)],
        compiler_params=_cparams(("parallel", "arbitrary")),
        name="norm_matmul",
    )(x, gain.reshape(1, K), w_bf16)


def _mlstm_kernel(qk_ref, v_ref, o_ref, if_ref, conv_ref, bias_ref, y_ref,
                  ext_ref, c_ref, n_ref, m_ref):
    L = MLSTM_CHUNK
    DK = MLSTM_DK
    step = pl.program_id(1)

    @pl.when(step == 0)
    def _():
        ext_ref[0:SUBLANES, :] = jnp.zeros((SUBLANES, 2 * MLSTM_QK), F32)
        c_ref[...] = jnp.zeros_like(c_ref)
        n_ref[...] = jnp.zeros_like(n_ref)
        m_ref[...] = jnp.zeros_like(m_ref)

    ext_ref[SUBLANES:SUBLANES + L, :] = qk_ref[0]
    acc = jnp.zeros((L, 2 * MLSTM_QK), F32)
    for j in range(MLSTM_CONV):
        off = SUBLANES - (MLSTM_CONV - 1) + j
        acc = acc + ext_ref[off:off + L, :] * conv_ref[j:j + 1, :]
    ext_ref[0:SUBLANES, :] = qk_ref[0, L - SUBLANES:L, :]
    qk = acc * jax.nn.sigmoid(acc)

    row = lax.broadcasted_iota(jnp.int32, (L, L), 0)
    col = lax.broadcasted_iota(jnp.int32, (L, L), 1)
    causal = col <= row
    eye = col == row

    def to_row(x_col):
        return jnp.sum(jnp.where(eye, x_col, 0.0), axis=0, keepdims=True)

    gates = if_ref[0]
    for h in range(MLSTM_HEADS):
        q = qk[:, h * DK:(h + 1) * DK] * (DK ** -0.5)
        k = qk[:, MLSTM_QK + h * DK:MLSTM_QK + (h + 1) * DK]
        v = v_ref[0, :, h * DK:(h + 1) * DK]
        ig_col = gates[:, h:h + 1] + bias_ref[0:1, h:h + 1]
        lf_col = jax.nn.log_sigmoid(
            gates[:, MLSTM_HEADS + h:MLSTM_HEADS + h + 1]
            + bias_ref[1:2, h:h + 1])
        lf_row = to_row(lf_col)
        ig_row = to_row(ig_col)
        b_col = jnp.sum(jnp.where(causal, lf_row, 0.0), axis=1, keepdims=True)
        b_row = to_row(b_col)
        b_last = b_col[L - 1:L, :]
        m_prev = m_ref[h:h + 1, 0:1]
        C = c_ref[h]
        n_row = n_ref[h:h + 1, :]

        dmat = jnp.where(causal, b_col - b_row + ig_row, -jnp.inf)
        inter = b_col + m_prev
        m_t = jnp.maximum(inter, jnp.max(dmat, axis=1, keepdims=True))
        qb = q.astype(BF16)
        kb = k.astype(BF16)
        vb = v.astype(BF16)
        s = lax.dot_general(qb, kb, (((1,), (1,)), ((), ())),
                            preferred_element_type=F32) * jnp.exp(dmat - m_t)
        w_inter = jnp.exp(inter - m_t)
        num = _dot(s.astype(BF16), vb) + w_inter * _dot(qb, C.astype(BF16))
        qn = jnp.sum(q * n_row, axis=1, keepdims=True)
        den = jnp.sum(s, axis=1, keepdims=True) + w_inter * qn
        hh = num / jnp.maximum(jnp.abs(den), jnp.exp(-m_t))
        o = o_ref[0, :, h * DK:(h + 1) * DK]
        y_ref[0, :, h * DK:(h + 1) * DK] = (jax.nn.sigmoid(o) * hh).astype(y_ref.dtype)

        g_end = b_last - b_col + ig_col
        m_new = jnp.maximum(b_last + m_prev, jnp.max(g_end, axis=0, keepdims=True))
        decay = jnp.exp(b_last + m_prev - m_new)
        ws = jnp.exp(g_end - m_new)
        kw = k * ws
        c_ref[h] = decay * C + lax.dot_general(
            kw.astype(BF16), vb, (((0,), (0,)), ((), ())), preferred_element_type=F32)
        n_ref[h:h + 1, :] = decay * n_row + jnp.sum(kw, axis=0, keepdims=True)
        m_ref[h:h + 1, :] = jnp.broadcast_to(m_new, (1, LANES))


def mlstm_branch(p, conv_w, b_i, b_f):
    B, T, _ = p.shape
    L = MLSTM_CHUNK
    bias = jnp.zeros((SUBLANES, LANES), F32)
    bias = bias.at[0, :MLSTM_HEADS].set(b_i).at[1, :MLSTM_HEADS].set(b_f)
    nqk = 2 * MLSTM_QK
    return pl.pallas_call(
        _mlstm_kernel,
        grid=(B, T // L),
        in_specs=[
            pl.BlockSpec((1, L, nqk), lambda b, c: (b, c, COL_QK // nqk)),
            pl.BlockSpec((1, L, MLSTM_V), lambda b, c: (b, c, COL_V // MLSTM_V)),
            pl.BlockSpec((1, L, MLSTM_V), lambda b, c: (b, c, COL_O // MLSTM_V)),
            pl.BlockSpec((1, L, LANES), lambda b, c: (b, c, COL_IF // LANES)),
            pl.BlockSpec((MLSTM_CONV, nqk), lambda b, c: (0, 0)),
            pl.BlockSpec((SUBLANES, LANES), lambda b, c: (0, 0)),
        ],
        out_specs=pl.BlockSpec((1, L, MLSTM_V), lambda b, c: (b, c, 0)),
        out_shape=jax.ShapeDtypeStruct((B, T, MLSTM_V), BF16),
        scratch_shapes=[
            pltpu.VMEM((SUBLANES + L, nqk), F32),
            pltpu.VMEM((MLSTM_HEADS, MLSTM_DK, MLSTM_DK), F32),
            pltpu.VMEM((SUBLANES, MLSTM_DK), F32),
            pltpu.VMEM((SUBLANES, LANES), F32),
        ],
        compiler_params=_cparams(("parallel", "arbitrary")),
        name="mlstm",
    )(p, p, p, p, conv_w, bias)


def _head_sum_matrix(group):
    r = lax.broadcasted_iota(jnp.int32, (LANES, LANES), 0) // group
    c = lax.broadcasted_iota(jnp.int32, (LANES, LANES), 1) // group
    return jnp.where(r == c, 1.0, 0.0).astype(BF16)


def _rwkv_pre_kernel(p_ref, prev_ref, mu_ref, vec_ref, w2_ref, a2_ref, g2_ref,
                     r_ref, w_ref, k_ref, v_ref, aa_ref, bb_ref, g_ref):
    W = RWKV_WIDTH
    tb = p_ref.shape[1]
    p = p_ref[0]
    last = prev_ref[0, SUBLANES - 1:SUBLANES, :]
    last = jnp.where(pl.program_id(1) == 0, 0.0, last)
    rid = lax.broadcasted_iota(jnp.int32, (tb, 1), 0)
    shifted = jnp.where(rid == 0, last, pltpu.roll(p, 1, 0))
    p = p + (shifted - p) * mu_ref[...]

    r = p[:, 0:W]
    k = p[:, W:2 * W]
    v = p[:, 2 * W:3 * W]
    wl = p[:, 3 * W:3 * W + LORA_PAD]
    al = p[:, 3 * W + LORA_PAD:3 * W + 2 * LORA_PAD]
    gl = p[:, 3 * W + 2 * LORA_PAD:]
    w0 = vec_ref[0:1, :]
    a0 = vec_ref[1:2, :]
    k_k = vec_ref[2:3, :]
    k_a = vec_ref[3:4, :]

    wx = -(w0 + _dot(jnp.tanh(wl).astype(BF16), w2_ref[...]))
    softplus = jnp.maximum(wx, 0.0) + jnp.log1p(jnp.exp(-jnp.abs(wx)))
    w = -softplus - 0.5
    log_decay = -jnp.exp(w)
    a = jax.nn.sigmoid(a0 + _dot(al.astype(BF16), a2_ref[...]))
    g = _dot(jax.nn.sigmoid(gl).astype(BF16), g2_ref[...])

    kk = k * k_k
    ones = _head_sum_matrix(RWKV_HEAD)
    sq = kk * kk
    ss = jnp.concatenate(
        [_dot_exact_rhs(sq[:, j * LANES:(j + 1) * LANES], ones) for j in range(W // LANES)],
        axis=1)
    kk = kk / jnp.maximum(jnp.sqrt(ss), L2_EPS)

    r_ref[0] = r
    w_ref[0] = log_decay
    k_ref[0] = k * (1.0 + (a - 1.0) * k_a)
    v_ref[0] = v
    aa_ref[0] = -kk
    bb_ref[0] = kk * a
    g_ref[0] = g


def rwkv_pre(p, mu, w0, w2, a0, a2, g2, k_k, k_a, *, tb=256):
    B, T, _ = p.shape
    W = RWKV_WIDTH
    tb = min(tb, T)
    mu_p = jnp.zeros((1, RWKV_PACKED), F32)
    mu_p = mu_p.at[0, :3 * W].set(mu[:3 * W])
    mu_p = mu_p.at[0, 3 * W:3 * W + W_LORA].set(mu[3 * W:3 * W + W_LORA])
    mu_p = mu_p.at[0, 3 * W + LORA_PAD:3 * W + LORA_PAD + A_LORA].set(
        mu[3 * W + W_LORA:3 * W + W_LORA + A_LORA])
    mu_p = mu_p.at[0, 3 * W + 2 * LORA_PAD:].set(mu[3 * W + W_LORA + A_LORA:])
    vec = jnp.zeros((SUBLANES, W), F32)
    vec = vec.at[0].set(w0).at[1].set(a0).at[2].set(k_k).at[3].set(k_a)
    w2_p = jnp.zeros((LORA_PAD, W), F32).at[:W_LORA].set(w2).astype(BF16)
    a2_p = jnp.zeros((LORA_PAD, W), F32).at[:A_LORA].set(a2).astype(BF16)
    nprev = tb // SUBLANES
    out = jax.ShapeDtypeStruct((B, T, W), F32)
    ospec = pl.BlockSpec((1, tb, W), lambda b, i: (b, i, 0))
    return pl.pallas_call(
        _rwkv_pre_kernel,
        grid=(B, T // tb),
        in_specs=[
            pl.BlockSpec((1, tb, RWKV_PACKED), lambda b, i: (b, i, 0)),
            pl.BlockSpec((1, SUBLANES, RWKV_PACKED),
                         lambda b, i: (b, jnp.maximum(i * nprev - 1, 0), 0)),
            pl.BlockSpec((1, RWKV_PACKED), lambda b, i: (0, 0)),
            pl.BlockSpec((SUBLANES, W), lambda b, i: (0, 0)),
            pl.BlockSpec((LORA_PAD, W), lambda b, i: (0, 0)),
            pl.BlockSpec((LORA_PAD, W), lambda b, i: (0, 0)),
            pl.BlockSpec((G_LORA, W), lambda b, i: (0, 0)),
        ],
        out_specs=[ospec] * 7,
        out_shape=[out] * 7,
        compiler_params=_cparams(("parallel", "parallel")),
        name="rwkv_pre",
    )(p, p, mu_p, vec, w2_p, a2_p, g2.astype(BF16))


RWKV_CHUNK = 64


def _mm3(a, b, dims):
    ah, al = _split2(a)
    bh, bl = _split2(b)

    def dg(x, y):
        return lax.dot_general(x, y, (dims, ((), ())), preferred_element_type=F32)

    return dg(ah, bh) + dg(al, bh) + dg(ah, bl)


_NN = ((1,), (0,))
_NT = ((1,), (1,))
_TN = ((0,), (0,))


def _rwkv_chunk_kernel(r_ref, lw_ref, k_ref, v_ref, aa_ref, bb_ref, q_ref, y0_ref, p_ref, g_ref):
    L = RWKV_CHUNK
    N = RWKV_HEAD
    W = RWKV_WIDTH
    npair = W // LANES
    row = lax.broadcasted_iota(jnp.int32, (L, LANES), 0)
    lane = lax.broadcasted_iota(jnp.int32, (L, LANES), 1)
    first = lane < N
    pos = lane % N
    strict = pos < row
    lower = pos <= row
    eye = pos == row

    def bd(x):
        return jnp.concatenate([jnp.where(first, x, 0.0), jnp.where(first, 0.0, x)], axis=0)

    def unbd(x):
        return jnp.where(first, x[0:L], x[L:2 * L])

    tri = jnp.where(lax.broadcasted_iota(jnp.int32, (L, L), 1)
                    <= lax.broadcasted_iota(jnp.int32, (L, L), 0), 1.0, 0.0).astype(BF16)
    lw = lw_ref[0]
    c = _dot_exact_rhs_left(tri, lw)
    c_last = c[L - 1:L, :]
    e_pos = jnp.exp(c)
    e_neg = jnp.exp(-c)
    e_hat = jnp.exp(c_last - c)
    a_all = aa_ref[0]
    b_all = bb_ref[0]
    k_all = k_ref[0]
    at_all = a_all * jnp.exp(c - lw)
    rt_all = r_ref[0] * e_pos
    bt_all = b_all * e_neg
    kt_all = k_all * e_neg
    bh_all = b_all * e_hat
    kh_all = k_all * e_hat
    gl_all = jnp.exp(c_last)

    def pair(x, q):
        return x[:, q * LANES:(q + 1) * LANES]

    pairs = range(npair)
    at = [pair(at_all, q) for q in pairs]
    rt = [pair(rt_all, q) for q in pairs]
    v = [pair(v_ref[0], q) for q in pairs]
    bt_bd = [bd(pair(bt_all, q)) for q in pairs]
    kt_bd = [bd(pair(kt_all, q)) for q in pairs]
    v_bd = [bd(x) for x in v]

    ar = [jnp.concatenate([at[q], rt[q]], axis=0) for q in pairs]
    sb = [_mm3(ar[q], bt_bd[q], _NT) for q in pairs]
    sk = [_mm3(ar[q], kt_bd[q], _NT) for q in pairs]
    aab = [jnp.where(strict, sb[q][0:L], 0.0) for q in pairs]
    mrb = [jnp.where(lower, sb[q][L:2 * L], 0.0) for q in pairs]
    aak_mrk = [jnp.concatenate([jnp.where(strict, sk[q][0:L], 0.0),
                                jnp.where(lower, sk[q][L:2 * L], 0.0)], axis=0) for q in pairs]

    t = [jnp.where(eye, 1.0, 0.0) + aab[q] for q in pairs]
    x = [_mm3(aab[q], bd(aab[q]), _NN) for q in pairs]
    n = 2
    while 2 * n < L:
        tx = [_mm3(jnp.concatenate([t[q], x[q]], axis=0), bd(x[q]), _NN) for q in pairs]
        t = [t[q] + tx[q][0:L] for q in pairs]
        x = [tx[q][L:2 * L] for q in pairs]
        n *= 2
    t = [t[q] + _mm3(t[q], bd(x[q]), _NN) for q in pairs]

    av_mv = [_mm3(aak_mrk[q], v_bd[q], _NN) for q in pairs]
    wu = [_mm3(t[q], jnp.concatenate([bd(at[q]), bd(av_mv[q][0:L])], axis=1), _NN)
          for q in pairs]
    w = [wu[q][:, 0:LANES] for q in pairs]
    u0 = [wu[q][:, LANES:2 * LANES] for q in pairs]
    qy = [_mm3(mrb[q], jnp.concatenate([bd(w[q]), bd(u0[q])], axis=1), _NN) for q in pairs]
    qq = [rt[q] + qy[q][:, 0:LANES] for q in pairs]
    y0 = [qy[q][:, LANES:2 * LANES] + av_mv[q][L:2 * L] for q in pairs]
    bh = [pair(bh_all, q) for q in pairs]
    kh = [pair(kh_all, q) for q in pairs]
    pg = [_mm3(bh[q], wu[q], _TN) for q in pairs]
    kv = [_mm3(kh[q], v[q], _TN) for q in pairs]
    pp = [unbd(pg[q][:, 0:LANES]) + jnp.where(eye, pair(gl_all, q), 0.0) for q in pairs]
    gg = [unbd(pg[q][:, LANES:2 * LANES]) + unbd(kv[q]) for q in pairs]
    for q in pairs:
        sl = slice(q * LANES, (q + 1) * LANES)
        q_ref[0, :, sl] = qq[q]
        y0_ref[0, :, sl] = y0[q]
        p_ref[0, :, sl] = pp[q]
        g_ref[0, :, sl] = gg[q]


def _dot_exact_rhs_left(ones_bf16, x):
    hi, mid, lo = _split3(x)
    return _dot(ones_bf16, hi) + _dot(ones_bf16, mid) + _dot(ones_bf16, lo)


def _rwkv_state_kernel(q_ref, y0_ref, p_ref, g_ref, y_ref, h_ref):
    L = RWKV_CHUNK
    N = RWKV_HEAD
    npair = RWKV_WIDTH // LANES

    @pl.when(pl.program_id(1) == 0)
    def _():
        h_ref[...] = jnp.zeros_like(h_ref)

    first = lax.broadcasted_iota(jnp.int32, (N, LANES), 1) < N

    def bd(x):
        return jnp.concatenate([jnp.where(first, x, 0.0), jnp.where(first, 0.0, x)], axis=0)

    pairs = range(npair)
    sls = [slice(q * LANES, (q + 1) * LANES) for q in pairs]
    h_bd = [bd(h_ref[:, sls[q]]) for q in pairs]
    y = [_mm3(q_ref[0, :, sls[q]], h_bd[q], _NN) for q in pairs]
    hn = [_mm3(p_ref[0, :, sls[q]], h_bd[q], _NN) for q in pairs]
    for q in pairs:
        y_ref[0, :, sls[q]] = y[q] + y0_ref[0, :, sls[q]]
        h_ref[:, sls[q]] = hn[q] + g_ref[0, :, sls[q]]


def _rwkv_state_post_kernel(q_ref, y0_ref, p_ref, g_ref, r_ref, k_ref, v_ref, gate_ref, vec_ref,
                            o_ref, h_ref):
    N = RWKV_HEAD
    npair = RWKV_WIDTH // LANES

    @pl.when(pl.program_id(1) == 0)
    def _():
        h_ref[...] = jnp.zeros_like(h_ref)

    first = lax.broadcasted_iota(jnp.int32, (N, LANES), 1) < N
    ones = _head_sum_matrix(N)

    def bd(x):
        return jnp.concatenate([jnp.where(first, x, 0.0), jnp.where(first, 0.0, x)], axis=0)

    pairs = range(npair)
    sls = [slice(q * LANES, (q + 1) * LANES) for q in pairs]
    h_bd = [bd(h_ref[:, sls[q]]) for q in pairs]
    y = [_mm3(q_ref[0, :, sls[q]], h_bd[q], _NN) + y0_ref[0, :, sls[q]] for q in pairs]
    hn = [_mm3(p_ref[0, :, sls[q]], h_bd[q], _NN) for q in pairs]
    for q in pairs:
        h_ref[:, sls[q]] = hn[q] + g_ref[0, :, sls[q]]
    mean = [_dot_exact_rhs(y[q], ones) * (1.0 / N) for q in pairs]
    d = [y[q] - mean[q] for q in pairs]
    var = [_dot_exact_rhs(d[q] * d[q], ones) * (1.0 / N) for q in pairs]
    rk = [_dot_exact_rhs(r_ref[0, :, sls[q]] * k_ref[0, :, sls[q]] * vec_ref[2:3, sls[q]], ones)
          for q in pairs]
    for q in pairs:
        yn = d[q] * lax.rsqrt(var[q] + GN_EPS) * vec_ref[0:1, sls[q]] + vec_ref[1:2, sls[q]]
        bonus = rk[q] * v_ref[0, :, sls[q]]
        o_ref[0, :, sls[q]] = ((yn + bonus) * gate_ref[0, :, sls[q]]).astype(o_ref.dtype)


def rwkv_chunked_post(r, lw, k, v, aa, bb, gate, ln_w, ln_b, r_k):
    B, T, W = r.shape
    L = RWKV_CHUNK
    spec = pl.BlockSpec((1, L, W), lambda b, c: (b, c, 0))
    out = jax.ShapeDtypeStruct((B, T, W), F32)
    q, y0, p, g = pl.pallas_call(
        _rwkv_chunk_kernel,
        grid=(B, T // L),
        in_specs=[spec] * 6,
        out_specs=[spec] * 4,
        out_shape=[out] * 4,
        compiler_params=_cparams(("parallel", "parallel")),
        name="rwkv_chunk",
    )(r, lw, k, v, aa, bb)
    vec = jnp.zeros((SUBLANES, W), F32)
    vec = vec.at[0].set(ln_w).at[1].set(ln_b).at[2].set(r_k.reshape(W))
    return pl.pallas_call(
        _rwkv_state_post_kernel,
        grid=(B, T // L),
        in_specs=[spec] * 8 + [pl.BlockSpec((SUBLANES, W), lambda b, c: (0, 0))],
        out_specs=spec,
        out_shape=jax.ShapeDtypeStruct((B, T, W), BF16),
        scratch_shapes=[pltpu.VMEM((RWKV_HEAD, W), F32)],
        compiler_params=_cparams(("parallel", "arbitrary")),
        name="rwkv_state_post",
    )(q, y0, p, g, r, k, v, gate, vec)


def rwkv_chunked(r, lw, k, v, aa, bb):
    B, T, W = r.shape
    L = RWKV_CHUNK
    spec = pl.BlockSpec((1, L, W), lambda b, c: (b, c, 0))
    out = jax.ShapeDtypeStruct((B, T, W), F32)
    q, y0, p, g = pl.pallas_call(
        _rwkv_chunk_kernel,
        grid=(B, T // L),
        in_specs=[spec] * 6,
        out_specs=[spec] * 4,
        out_shape=[out] * 4,
        compiler_params=_cparams(("parallel", "parallel")),
        name="rwkv_chunk",
    )(r, lw, k, v, aa, bb)
    return pl.pallas_call(
        _rwkv_state_kernel,
        grid=(B, T // L),
        in_specs=[spec] * 4,
        out_specs=spec,
        out_shape=out,
        scratch_shapes=[pltpu.VMEM((RWKV_HEAD, W), F32)],
        compiler_params=_cparams(("parallel", "arbitrary")),
        name="rwkv_state",
    )(q, y0, p, g)


def _rwkv_post_kernel(y_ref, r_ref, k_ref, v_ref, g_ref, vec_ref, o_ref):
    W = RWKV_WIDTH
    ones = _head_sum_matrix(RWKV_HEAD)
    ln_w = vec_ref[0:1, :]
    ln_b = vec_ref[1:2, :]
    r_k = vec_ref[2:3, :]

    def head_sum(x):
        return jnp.concatenate(
            [_dot_exact_rhs(x[:, j * LANES:(j + 1) * LANES], ones) for j in range(W // LANES)],
            axis=1)

    y = y_ref[...]
    mean = head_sum(y) * (1.0 / RWKV_HEAD)
    d = y - mean
    var = head_sum(d * d) * (1.0 / RWKV_HEAD)
    yn = d * lax.rsqrt(var + GN_EPS) * ln_w + ln_b
    bonus = head_sum(r_ref[...] * k_ref[...] * r_k) * v_ref[...]
    o_ref[...] = ((yn + bonus) * g_ref[...]).astype(o_ref.dtype)


def rwkv_post(y, r, k, v, g, ln_w, ln_b, r_k, *, tb=256):
    M, W = y.shape
    tb = min(tb, M)
    vec = jnp.zeros((SUBLANES, W), F32)
    vec = vec.at[0].set(ln_w).at[1].set(ln_b).at[2].set(r_k.reshape(W))
    spec = pl.BlockSpec((tb, W), lambda i: (i, 0))
    return pl.pallas_call(
        _rwkv_post_kernel,
        grid=(M // tb,),
        in_specs=[spec] * 5 + [pl.BlockSpec((SUBLANES, W), lambda i: (0, 0))],
        out_specs=spec,
        out_shape=jax.ShapeDtypeStruct((M, W), BF16),
        compiler_params=_cparams(("parallel",)),
        name="rwkv_post",
    )(y, r, k, v, g, vec)


def _merge_kernel(ym_ref, yr_ref, gm_ref, gr_ref, pm_ref, pr_ref, o_ref):
    m = _dot(ym_ref[...], pm_ref[...])
    r = _dot(yr_ref[...], pr_ref[...])
    o_ref[...] = (jax.nn.sigmoid(gm_ref[...]) * m
                  + jax.nn.sigmoid(gr_ref[...]) * r).astype(o_ref.dtype)


def merge(y_m, y_r, gates, proj_m, proj_r, *, tm=512, tn=2048):
    M, K = y_m.shape
    D = proj_m.shape[1]
    tm = min(tm, M)
    return pl.pallas_call(
        _merge_kernel,
        grid=(M // tm, D // tn),
        in_specs=[
            pl.BlockSpec((tm, K), lambda i, j: (i, 0)),
            pl.BlockSpec((tm, K), lambda i, j: (i, 0)),
            pl.BlockSpec((tm, tn), lambda i, j: (i, j)),
            pl.BlockSpec((tm, tn), lambda i, j: (i, D // tn + j)),
            pl.BlockSpec((K, tn), lambda i, j: (0, j)),
            pl.BlockSpec((K, tn), lambda i, j: (0, j)),
        ],
        out_specs=pl.BlockSpec((tm, tn), lambda i, j: (i, j)),
        out_shape=jax.ShapeDtypeStruct((M, D), BF16),
        compiler_params=_cparams(("parallel", "arbitrary")),
        name="merge",
    )(y_m, y_r, gates, gates, proj_m, proj_r)


def _out_proj_kernel(mix_ref, x_ref, w_ref, g_ref, h_ref, xn_ref):
    h = x_ref[...] + _dot(mix_ref[...], w_ref[...])
    h_ref[...] = h
    ms = jnp.mean(h * h, axis=-1, keepdims=True)
    xn_ref[...] = h * lax.rsqrt(ms + RMS_EPS) * g_ref[...]


def out_proj(mixed, x2d, w_out_bf16, gain, *, tm=256):
    M, D = x2d.shape
    tm = min(tm, M)
    spec = pl.BlockSpec((tm, D), lambda i: (i, 0))
    return pl.pallas_call(
        _out_proj_kernel,
        grid=(M // tm,),
        in_specs=[spec, spec, pl.BlockSpec((D, D), lambda i: (0, 0)),
                  pl.BlockSpec((1, D), lambda i: (0, 0))],
        out_specs=[spec, spec],
        out_shape=[jax.ShapeDtypeStruct((M, D), F32)] * 2,
        compiler_params=_cparams(("parallel",)),
        name="out_proj",
    )(mixed, x2d, w_out_bf16, gain.reshape(1, D))


def _matmul_split_kernel(a_ref, whi_ref, wlo_ref, o_ref, hi_ref, lo_ref):
    @pl.when(pl.program_id(1) == 0)
    def _():
        hi, lo = _split2(a_ref[...])
        hi_ref[...] = hi
        lo_ref[...] = lo

    o_ref[...] = (_dot(hi_ref[...], whi_ref[...]) + _dot(lo_ref[...], whi_ref[...])
                  + _dot(hi_ref[...], wlo_ref[...]))


def matmul_split(a, w, *, tm=512, tn=512):
    M, K = a.shape
    N = w.shape[1]
    tm = min(tm, M)
    whi = w.astype(BF16)
    wlo = (w - whi.astype(F32)).astype(BF16)
    return pl.pallas_call(
        _matmul_split_kernel,
        grid=(M // tm, N // tn),
        in_specs=[pl.BlockSpec((tm, K), lambda i, j: (i, 0)),
                  pl.BlockSpec((K, tn), lambda i, j: (0, j)),
                  pl.BlockSpec((K, tn), lambda i, j: (0, j))],
        out_specs=pl.BlockSpec((tm, tn), lambda i, j: (i, j)),
        out_shape=jax.ShapeDtypeStruct((M, N), F32),
        scratch_shapes=[pltpu.VMEM((tm, K), BF16), pltpu.VMEM((tm, K), BF16)],
        compiler_params=_cparams(("parallel", "arbitrary")),
        name="peer_query",
    )(a, whi, wlo)


def _route_kernel(q_ref, khi_ref, klo_ref, idx_ref, gate_ref,
                  s_ref, tv_ref, tp_ref, c_ref, e_ref, bv_ref, bi_ref):
    tn = q_ref.shape[0]
    KK = PEER_TOPK
    half = D_KEY // 2
    NEG = -jnp.inf
    key_id = lax.broadcasted_iota(jnp.int32, (N_KEYS, tn), 0).astype(F32)
    crow = lax.broadcasted_iota(jnp.int32, (PEER_CAND, tn), 0)
    mid = crow - KK
    cand_id = jnp.where(
        crow < KK, crow,
        jnp.where(crow < PEER_CAND - SUBLANES,
                  (mid // SUBLANES + 1) * KK + mid % SUBLANES,
                  (crow - (PEER_CAND - SUBLANES) + SUBLANES) * KK)).astype(F32)
    rank = lax.broadcasted_iota(jnp.int32, (KK, tn), 0)

    def nt(a, b):
        return lax.dot_general(a, b, (((1,), (1,)), ((), ())), preferred_element_type=F32)

    for hp in range(2 * PEER_HEADS):
        qhi, qlo = _split2(q_ref[:, hp * half:(hp + 1) * half])
        p = hp % 2
        s_ref[hp] = nt(khi_ref[p], qhi) + nt(khi_ref[p], qlo) + nt(klo_ref[p], qhi)
    tv_ref[...] = jnp.zeros_like(tv_ref)
    tp_ref[...] = jnp.zeros_like(tp_ref)
    bv_ref[...] = jnp.zeros_like(bv_ref)
    bi_ref[...] = jnp.zeros_like(bi_ref)

    def sub_topk(kk, carry):
        for hp in range(2 * PEER_HEADS):
            s = s_ref[hp]
            m = jnp.max(s, axis=0, keepdims=True)
            pos = jnp.min(jnp.where(s == m, key_id, float(N_KEYS)), axis=0, keepdims=True)
            s_ref[hp] = jnp.where(key_id == pos, NEG, s)
            tv_ref[hp] = jnp.where(rank == kk, m, tv_ref[hp])
            tp_ref[hp] = jnp.where(rank == kk, pos, tp_ref[hp])
        return carry

    lax.fori_loop(0, KK, sub_topk, 0)

    grp = lax.broadcasted_iota(jnp.int32, (SUBLANES, tn), 0)
    for h in range(PEER_HEADS):
        v1, v2 = tv_ref[2 * h], tv_ref[2 * h + 1]
        p1, p2 = tp_ref[2 * h], tp_ref[2 * h + 1]
        c_ref[h, 0:KK, :] = v1[0:1, :] + v2
        e_ref[h, 0:KK, :] = p1[0:1, :] * float(N_KEYS) + p2
        for i in range(1, SUBLANES):
            keep = grp < KK // (i + 1)
            rows = slice(KK + (i - 1) * SUBLANES, KK + i * SUBLANES)
            c_ref[h, rows, :] = jnp.where(keep, v1[i:i + 1, :] + v2[0:SUBLANES, :], NEG)
            e_ref[h, rows, :] = jnp.where(
                keep, p1[i:i + 1, :] * float(N_KEYS) + p2[0:SUBLANES, :], -1.0)
        c_ref[h, PEER_CAND - SUBLANES:PEER_CAND, :] = v1[SUBLANES:KK, :] + v2[0:1, :]
        e_ref[h, PEER_CAND - SUBLANES:PEER_CAND, :] = p1[SUBLANES:KK, :] * float(N_KEYS) + p2[0:1, :]

    def cand_topk(kk, carry):
        for h in range(PEER_HEADS):
            c = c_ref[h]
            m = jnp.max(c, axis=0, keepdims=True)
            pos = jnp.min(jnp.where(c == m, cand_id, float(KK * KK)), axis=0, keepdims=True)
            hit = cand_id == pos
            eid = jnp.max(jnp.where(hit, e_ref[h], -1.0), axis=0, keepdims=True)
            c_ref[h] = jnp.where(hit, NEG, c)
            rows = slice(h * KK, (h + 1) * KK)
            bv_ref[rows, :] = jnp.where(rank == kk, m, bv_ref[rows, :])
            bi_ref[rows, :] = jnp.where(rank == kk, eid, bi_ref[rows, :])
        return carry

    lax.fori_loop(0, KK, cand_topk, 0)

    for h in range(PEER_HEADS):
        rows = slice(h * KK, (h + 1) * KK)
        b = bv_ref[rows, :]
        e = jnp.exp(b - b[0:1, :])
        bv_ref[rows, :] = e / jnp.sum(e, axis=0, keepdims=True)
    gate_ref[...] = bv_ref[...].T
    idx_ref[...] = bi_ref[...].T.astype(jnp.int32)


def peer_route(q, sub_keys, *, tn=128):
    M = q.shape[0]
    tn = min(tn, M)
    khi = sub_keys.astype(BF16)
    klo = (sub_keys - khi.astype(F32)).astype(BF16)
    kspec = pl.BlockSpec((2, N_KEYS, D_KEY // 2), lambda i: (0, 0, 0))
    ospec = pl.BlockSpec((tn, N_KEYS), lambda i: (i, 0))
    return pl.pallas_call(
        _route_kernel,
        grid=(M // tn,),
        in_specs=[pl.BlockSpec((tn, PEER_HEADS * D_KEY), lambda i: (i, 0)), kspec, kspec],
        out_specs=[ospec, ospec],
        out_shape=[jax.ShapeDtypeStruct((M, N_KEYS), jnp.int32),
                   jax.ShapeDtypeStruct((M, N_KEYS), F32)],
        scratch_shapes=[
            pltpu.VMEM((2 * PEER_HEADS, N_KEYS, tn), F32),
            pltpu.VMEM((2 * PEER_HEADS, PEER_TOPK, tn), F32),
            pltpu.VMEM((2 * PEER_HEADS, PEER_TOPK, tn), F32),
            pltpu.VMEM((PEER_HEADS, PEER_CAND, tn), F32),
            pltpu.VMEM((PEER_HEADS, PEER_CAND, tn), F32),
            pltpu.VMEM((PEER_HEADS * PEER_TOPK, tn), F32),
            pltpu.VMEM((PEER_HEADS * PEER_TOPK, tn), F32),
        ],
        compiler_params=_cparams(("parallel",)),
        name="peer_route",
    )(q, khi, klo)


def _gelu_exact(x):
    return 0.5 * x * (1.0 + lax.erf(x * (2.0 ** -0.5)))


def _peer_expert_kernel(idx_ref, gate_ref, xn_ref, h_ref, gain_ref, uv_ref, o_ref,
                        buf_even_ref, buf_odd_ref, acc_ref, sem):
    tb, D = xn_ref.shape
    E = idx_ref.shape[2]
    nt = D // LANES
    step = pl.program_id(0)

    def staging(slot):
        return (buf_odd_ref if slot % 2 else buf_even_ref).at[slot // 2]

    def slot_copy(slot):
        return pltpu.make_async_copy(staging(slot), staging(slot), sem.at[slot])

    def issue(t, slot):
        for e in range(E):
            pltpu.make_async_copy(uv_ref.at[idx_ref[0, t, e]], staging(slot).at[:, e, :],
                                  sem.at[slot]).start(priority=e % 2)

    @pl.when(step == 0)
    def _():
        for t in range(PEER_LOOKAHEAD):
            issue(t, t)

    eye = (lax.broadcasted_iota(jnp.int32, (E, E), 0)
           == lax.broadcasted_iota(jnp.int32, (E, E), 1))

    def body(t8, carry):
        base = pl.multiple_of(t8 * SUBLANES, SUBLANES)
        x8 = xn_ref[pl.ds(base, SUBLANES), :]
        g8 = gate_ref[pl.ds(base, SUBLANES), :]
        rows = []
        for j in range(SUBLANES):
            slot = j % PEER_SLOTS
            issue(base + j + PEER_LOOKAHEAD, (j + PEER_LOOKAHEAD) % PEER_SLOTS)
            slot_copy(slot).wait()
            part = jnp.zeros((E, LANES), F32)
            for s in range(nt):
                part = part + staging(slot)[s] * x8[j:j + 1, s * LANES:(s + 1) * LANES]
            act = jnp.sum(part, axis=1, keepdims=True)
            g_col = jnp.sum(jnp.where(eye, g8[j:j + 1], 0.0), axis=1, keepdims=True)
            c = g_col * _gelu_exact(act)
            rows.append(jnp.concatenate(
                [jnp.sum(staging(slot)[nt + s] * c, axis=0, keepdims=True) for s in range(nt)],
                axis=1))
        acc_ref[pl.ds(base, SUBLANES), :] = jnp.concatenate(rows, axis=0)
        return carry

    lax.fori_loop(0, tb // SUBLANES, body, 0)

    @pl.when(step == pl.num_programs(0) - 1)
    def _():
        for t in range(PEER_LOOKAHEAD):
            slot_copy(t % PEER_SLOTS).wait()

    hh = h_ref[...] + acc_ref[...]
    ms = jnp.mean(hh * hh, axis=-1, keepdims=True)
    o_ref[...] = hh * lax.rsqrt(ms + RMS_EPS) * gain_ref[...]


def pack_expert_rows(u, v):
    n, d = u.shape
    nt = d // LANES
    tn = 256

    def pack_kernel(u_ref, v_ref, o_ref):
        for s in range(nt):
            o_ref[:, s, :] = u_ref[:, s * LANES:(s + 1) * LANES]
            o_ref[:, nt + s, :] = v_ref[:, s * LANES:(s + 1) * LANES]

    return pl.pallas_call(
        pack_kernel,
        grid=(n // tn,),
        in_specs=[pl.BlockSpec((tn, d), lambda i: (i, 0))] * 2,
        out_specs=pl.BlockSpec((tn, 2 * nt, LANES), lambda i: (i, 0, 0)),
        out_shape=jax.ShapeDtypeStruct((n, 2 * nt, LANES), F32),
        compiler_params=_cparams(("parallel",)),
        name="peer_pack",
    )(u, v)


def peer_experts(idx, gate, xn, h, gain, uv, *, tb=64):
    M, D = xn.shape
    E = idx.shape[1]
    tb = min(tb, M)
    assert tb % PEER_SLOTS == 0 and M % tb == 0
    nb = M // tb
    idx3 = idx.reshape(nb, tb, E)
    ahead = jnp.concatenate([idx3[1:, :PEER_SLOTS], idx3[-1:, :PEER_SLOTS]], axis=0)
    idx_ext = jnp.concatenate([idx3, ahead], axis=1)
    spec = pl.BlockSpec((tb, D), lambda i: (i, 0))
    return pl.pallas_call(
        _peer_expert_kernel,
        grid=(nb,),
        in_specs=[
            pl.BlockSpec((1, tb + PEER_SLOTS, E), lambda i: (i, 0, 0), memory_space=pltpu.SMEM),
            pl.BlockSpec((tb, E), lambda i: (i, 0)),
            spec, spec,
            pl.BlockSpec((1, D), lambda i: (0, 0)),
            pl.BlockSpec(memory_space=pl.ANY),
        ],
        out_specs=spec,
        out_shape=jax.ShapeDtypeStruct((M, D), F32),
        scratch_shapes=[
            pltpu.VMEM((PEER_SLOTS // 2, 2 * D // LANES, E, LANES), F32),
            pltpu.VMEM((PEER_SLOTS // 2, 2 * D // LANES, E, LANES), F32),
            pltpu.VMEM((tb, D), F32),
            pltpu.SemaphoreType.DMA((PEER_SLOTS,)),
        ],
        compiler_params=_cparams(("arbitrary",)),
        name="peer_experts",
    )(idx_ext, gate, xn, h, gain.reshape(1, D), uv)


def _pack_w_in(w_in):
    D = w_in.shape[0]
    W = RWKV_WIDTH
    o = 0
    qk = w_in[:, o:o + 2 * MLSTM_QK]; o += 2 * MLSTM_QK
    v = w_in[:, o:o + MLSTM_V]; o += MLSTM_V
    og = w_in[:, o:o + MLSTM_V]; o += MLSTM_V
    ifg = w_in[:, o:o + 2 * MLSTM_HEADS]; o += 2 * MLSTM_HEADS
    rkv = w_in[:, o:o + 3 * W]; o += 3 * W
    wl = w_in[:, o:o + W_LORA]; o += W_LORA
    al = w_in[:, o:o + A_LORA]; o += A_LORA
    gl = w_in[:, o:o + G_LORA]; o += G_LORA
    gates = w_in[:, o:]

    def pad(t, n):
        return jnp.pad(t, ((0, 0), (0, n - t.shape[1])))

    w_mlstm = jnp.concatenate([qk, v, og, pad(ifg, LANES)], axis=1)
    w_rwkv = jnp.concatenate([rkv, pad(wl, LORA_PAD), pad(al, LORA_PAD), gl], axis=1)
    assert w_mlstm.shape == (D, MLSTM_PACKED) and w_rwkv.shape == (D, RWKV_PACKED)
    return w_mlstm.astype(BF16), w_rwkv.astype(BF16), gates.astype(BF16)


def kernel(x, norm_mix_gain, w_in, mlstm_conv, mlstm_b_i, mlstm_b_f, rwkv_mu, rwkv_w0, rwkv_w2,
           rwkv_a0, rwkv_a2, rwkv_g2, rwkv_k_k, rwkv_k_a, rwkv_r_k, rwkv_ln_w, rwkv_ln_b,
           proj_mlstm, proj_rwkv, w_out, norm_ffn_gain, peer_w_query, peer_sub_keys,
           peer_u, peer_v, norm_final_gain):
    B, T, D = x.shape
    assert w_in.shape[0] == 1, "the output norm is fused into the single layer's PEER kernel"
    l = 0
    x2d = x.reshape(B * T, D)
    w_mlstm, w_rwkv, w_gates = _pack_w_in(w_in[l])
    p_m = norm_matmul(x2d, norm_mix_gain[l], w_mlstm, tn=MLSTM_PACKED // 3).reshape(B, T, MLSTM_PACKED)
    p_r = norm_matmul(x2d, norm_mix_gain[l], w_rwkv, tn=RWKV_PACKED // 4).reshape(B, T, RWKV_PACKED)
    p_g = norm_matmul(x2d, norm_mix_gain[l], w_gates, tn=1024)

    y_m = mlstm_branch(p_m, mlstm_conv[l], mlstm_b_i[l], mlstm_b_f[l])
    r, w, k, v, aa, bb, g = rwkv_pre(p_r, rwkv_mu[l], rwkv_w0[l], rwkv_w2[l], rwkv_a0[l],
                                     rwkv_a2[l], rwkv_g2[l], rwkv_k_k[l], rwkv_k_a[l])
    y_r = rwkv_chunked_post(r, w, k, v, aa, bb, g, rwkv_ln_w[l], rwkv_ln_b[l],
                            rwkv_r_k[l]).reshape(B * T, RWKV_WIDTH)
    mixed = merge(y_m.reshape(B * T, MLSTM_V), y_r, p_g,
                  proj_mlstm[l].astype(BF16), proj_rwkv[l].astype(BF16))
    h2d, xn = out_proj(mixed, x2d, w_out[l].astype(BF16), norm_ffn_gain[l])
    q = matmul_split(xn, peer_w_query[l])
    idx, gate = peer_route(q, peer_sub_keys[l])
    out = peer_experts(idx, gate, xn, h2d, norm_final_gain, pack_expert_rows(peer_u[l], peer_v[l]))
    return out.reshape(B, T, D)
```
